```python
import math
import jax, jax.numpy as jnp
from jax import lax
import numpy as np

D_MODEL = 1024
BATCH = 4
SEQ = 8192
DEPTH = 2

N_A = DEPTH // 2
N_B = DEPTH - N_A

CONV_CH = D_MODEL
CONV_W = 3

HEAD_DIM = 64
N_HEADS = D_MODEL // HEAD_DIM
N_KV_HEADS = 4
GROUP = N_HEADS // N_KV_HEADS
WINDOW = 128
BLOCK = 128
ROT_DIM = HEAD_DIM // 4
ROPE_THETA = 500000.0
EPS = 1e-6
NEG_INF = -1e30

kernel_name = "yoco_shortconv_swa_sink_hybrid"


def rmsnorm(x, g):
    xf = x.astype(jnp.float32)
    y = xf * lax.rsqrt(jnp.mean(xf * xf, axis=-1, keepdims=True) + EPS)
    return (y * g.astype(jnp.float32)).astype(x.dtype)


def rope_tables(seq):
    pos = jnp.arange(seq, dtype=jnp.float32)
    inv = ROPE_THETA ** (-jnp.arange(0, ROT_DIM, 2, dtype=jnp.float32) / ROT_DIM)
    ang = pos[:, None] * inv[None, :]
    return jnp.cos(ang), jnp.sin(ang)


def partial_rope(t, cos, sin):
    tf = t.astype(jnp.float32)
    rot, rest = tf[..., :ROT_DIM], tf[..., ROT_DIM:]
    r1, r2 = rot[..., : ROT_DIM // 2], rot[..., ROT_DIM // 2:]
    c = cos[None, :, None, :]
    s = sin[None, :, None, :]
    out = jnp.concatenate([r1 * c - r2 * s, r2 * c + r1 * s, rest], axis=-1)
    return out.astype(t.dtype)


def short_conv_mixer(x, ln_g, w_in, conv_w, w_out):
    h = rmsnorm(x, ln_g)
    b_gate, c_gate, u, z = jnp.split(h @ w_in, 4, axis=-1)
    v = c_gate * u
    conv = lax.conv_general_dilated(
        v, conv_w.astype(v.dtype), window_strides=(1,), padding=[(CONV_W - 1, 0)],
        dimension_numbers=("NWC", "WIO", "NWC"), feature_group_count=CONV_CH)
    y = b_gate * conv * jax.nn.silu(z)
    return y @ w_out


def shared_kv(x, ln_g, w_kv, k_norm_g, cos, sin):
    bsz, seq, _ = x.shape
    nb = seq // BLOCK
    h = rmsnorm(x, ln_g)
    k, v = jnp.split(h @ w_kv, 2, axis=-1)
    k = k.reshape(bsz, seq, N_KV_HEADS, HEAD_DIM)
    v = v.reshape(bsz, seq, N_KV_HEADS, HEAD_DIM)
    k = partial_rope(rmsnorm(k, k_norm_g), cos, sin)

    def band(t):
        tb = t.reshape(bsz, nb, BLOCK, N_KV_HEADS, HEAD_DIM)
        prev = jnp.concatenate([jnp.zeros_like(tb[:, :1]), tb[:, :-1]], axis=1)
        tb = jnp.concatenate([prev, tb], axis=2)
        return jnp.moveaxis(tb, 1, 0)

    return band(k), band(v)


def swa_sink_mixer(x, ln_g, w_in, q_norm_g, sinks, w_out, k_band, v_band, cos, sin):
    bsz, seq, _ = x.shape
    nb = seq // BLOCK
    h = rmsnorm(x, ln_g)
    q, z = jnp.split(h @ w_in, 2, axis=-1)
    q = q.reshape(bsz, seq, N_HEADS, HEAD_DIM)
    q = partial_rope(rmsnorm(q, q_norm_g), cos, sin)
    q = q.reshape(bsz, nb, BLOCK, N_KV_HEADS, GROUP, HEAD_DIM)
    q = jnp.moveaxis(q, 1, 0)

    scale = 1.0 / math.sqrt(HEAD_DIM)
    a_idx = jnp.arange(BLOCK)[:, None]
    c_idx = jnp.arange(2 * BLOCK)[None, :]
    rel_ok = (c_idx <= a_idx + BLOCK) & (c_idx > a_idx + BLOCK - WINDOW)
    sink = sinks.astype(jnp.float32).reshape(N_KV_HEADS, GROUP)[None, :, :, None, None]

    def block_attn(args):
        qb, kb, vb, blk = args
        s = jnp.einsum("bqkgd,bckd->bkgqc", qb.astype(jnp.float32),
                       kb.astype(jnp.float32)) * scale
        mask = rel_ok & ((blk > 0) | (c_idx >= BLOCK))
        s = jnp.where(mask[None, None, None], s, NEG_INF)
        m = jnp.maximum(jnp.max(s, axis=-1, keepdims=True), sink)
        p = jnp.exp(s - m)
        denom = jnp.sum(p, axis=-1, keepdims=True) + jnp.exp(sink - m)
        o = jnp.einsum("bkgqc,bckd->bqkgd", p / denom, vb.astype(jnp.float32))
        return o.astype(x.dtype)

    o = lax.map(block_attn, (q, k_band, v_band, jnp.arange(nb)))
    o = jnp.moveaxis(o, 0, 1).reshape(bsz, seq, N_HEADS * HEAD_DIM)
    return (o * jax.nn.silu(z)) @ w_out


def setup_inputs(seed: int = 0) -> dict:
    key = jax.random.key(seed)
    ks = jax.random.split(key, 16)
    f32 = jnp.float32
    D, W, H, KV, HD = D_MODEL, CONV_CH, N_HEADS, N_KV_HEADS, HEAD_DIM
    nrm = lambda k, shape, fan: jax.random.normal(k, shape, f32) * (fan ** -0.5)
    return {
        "x": jax.random.normal(ks[0], (BATCH, SEQ, D), f32),
        "ln_a": 1.0 + 0.02 * jax.random.normal(ks[1], (N_A, D), f32),
        "w_in_a": nrm(ks[2], (N_A, D, 4 * W), D),
        "conv_a": nrm(ks[3], (N_A, CONV_W, 1, W), CONV_W),
        "w_out_a": nrm(ks[4], (N_A, W, D), W),
        "ln_kv": 1.0 + 0.02 * jax.random.normal(ks[5], (D,), f32),
        "w_kv": nrm(ks[6], (D, 2 * KV * HD), D),
        "k_norm": 1.0 + 0.02 * jax.random.normal(ks[7], (HD,), f32),
        "ln_b": 1.0 + 0.02 * jax.random.normal(ks[8], (N_B, D), f32),
        "w_in_b": nrm(ks[9], (N_B, D, 2 * H * HD), D),
        "q_norm": 1.0 + 0.02 * jax.random.normal(ks[10], (N_B, HD), f32),
        "sinks": 0.5 * jax.random.normal(ks[11], (N_B, H), f32),
        "w_out_b": nrm(ks[12], (N_B, H * HD, D), H * HD),
    }


def reference(x, ln_a, w_in_a, conv_a, w_out_a, ln_kv, w_kv, k_norm,
              ln_b, w_in_b, q_norm, sinks, w_out_b):
    seq = x.shape[1]
    cos, sin = rope_tables(seq)
    k_band = v_band = None
    for layer in range(DEPTH):
        if layer < N_A:
            i = layer
            x = x + short_conv_mixer(x, ln_a[i], w_in_a[i], conv_a[i], w_out_a[i])
        else:
            i = layer - N_A
            if i == 0:
                k_band, v_band = shared_kv(x, ln_kv, w_kv, k_norm, cos, sin)
            x = x + swa_sink_mixer(x, ln_b[i], w_in_b[i], q_norm[i], sinks[i], w_out_b[i],
                                   k_band, v_band, cos, sin)
    return x
```

```python
import functools
import math

import jax
import jax.numpy as jnp
from jax import lax
from jax.experimental import pallas as pl
from jax.experimental.pallas import tpu as pltpu

F32 = jnp.float32
BF16 = jnp.bfloat16

HEAD_DIM = 64
N_KV_HEADS = 4
GROUP = 4
N_HEADS = N_KV_HEADS * GROUP
WINDOW = 128
BLOCK = 128
ROT_DIM = HEAD_DIM // 4
ROT_HALF = ROT_DIM // 2
ROPE_THETA = 500000.0
EPS = 1e-6
NEG_INF = -1e30
CONV_W = 3

LANES = 128
SUBLANES = 8
VMEM_LIMIT_BYTES = 56 * 1024 * 1024

SEQ_TILE = 512
CHUNK = 256


def _rms_scale(x):
    return lax.rsqrt(jnp.mean(x * x, axis=-1, keepdims=True) + EPS)


def _layer_a_kernel(x_ref, g_ref, win_ref, cw_ref, wout_ref, o_ref, vbuf, *, width):
    w = width
    n_chunks = x_ref.shape[0] // CHUNK

    @pl.when(pl.program_id(1) == 0)
    def _():
        vbuf[0:SUBLANES, :] = jnp.zeros((SUBLANES, w), F32)

    def chunk(i, carry):
        r0 = pl.multiple_of(i * CHUNK, CHUNK)
        x = x_ref[pl.ds(r0, CHUNK), :]
        h = (x * _rms_scale(x) * g_ref[...]).astype(BF16)
        proj = jnp.dot(h, win_ref[...], preferred_element_type=F32)
        b_gate = proj[:, 0:w]
        c_gate = proj[:, w:2 * w]
        u = proj[:, 2 * w:3 * w]
        z = proj[:, 3 * w:4 * w]
        vbuf[SUBLANES:SUBLANES + CHUNK, :] = c_gate * u
        conv = (cw_ref[2:3, :] * vbuf[SUBLANES:SUBLANES + CHUNK, :]
                + cw_ref[1:2, :] * vbuf[SUBLANES - 1:SUBLANES - 1 + CHUNK, :]
                + cw_ref[0:1, :] * vbuf[SUBLANES - 2:SUBLANES - 2 + CHUNK, :])
        vbuf[0:SUBLANES, :] = vbuf[CHUNK:CHUNK + SUBLANES, :]
        y = (b_gate * conv * (z * jax.nn.sigmoid(z))).astype(BF16)
        o_ref[pl.ds(r0, CHUNK), :] = x + jnp.dot(y, wout_ref[...], preferred_element_type=F32)
        return carry

    lax.fori_loop(0, n_chunks, chunk, 0)


def _layer_a(x, ln_g, w_in, conv_w, w_out):
    bsz, seq, d = x.shape
    w = w_out.shape[0]
    const = lambda b, j: (0, 0)
    single = pl.Buffered(1)
    return pl.pallas_call(
        functools.partial(_layer_a_kernel, width=w),
        out_shape=jax.ShapeDtypeStruct(x.shape, x.dtype),
        grid=(bsz, seq // SEQ_TILE),
        in_specs=[
            pl.BlockSpec((None, SEQ_TILE, d), lambda b, j: (b, j, 0)),
            pl.BlockSpec((1, d), const),
            pl.BlockSpec((d, 4 * w), const, pipeline_mode=single),
            pl.BlockSpec((CONV_W, w), const),
            pl.BlockSpec((w, d), const, pipeline_mode=single),
        ],
        out_specs=pl.BlockSpec((None, SEQ_TILE, d), lambda b, j: (b, j, 0)),
        scratch_shapes=[pltpu.VMEM((CHUNK + SUBLANES, w), F32)],
        compiler_params=pltpu.CompilerParams(
            dimension_semantics=("arbitrary", "arbitrary"),
            vmem_limit_bytes=VMEM_LIMIT_BYTES),
        name="layer_a_shortconv",
    )(x, ln_g.reshape(1, d), w_in.astype(BF16), conv_w.reshape(CONV_W, w), w_out.astype(BF16))


def _softmax_half(s, bias, sink):
    sm = jnp.minimum(s, bias)
    m = jnp.maximum(jnp.max(sm, axis=-1, keepdims=True), sink)
    p = jnp.exp(sm - m)
    denom = jnp.sum(p, axis=-1, keepdims=True) + jnp.exp(sink - m)
    return p.astype(BF16), 1.0 / denom


def _layer_b_kernel(x_ref, lnkv_ref, lnb_ref, wkv_ref, kg_ref, cost_ref, sint_ref,
                    winb_ref, qg_ref, cq_ref, s1_ref, s2_ref, sink_ref, bias_ref, woutb_ref,
                    o_ref, kt_scr, v_scr, attn_scr):
    d_model = x_ref.shape[1]
    n_chunks = x_ref.shape[0] // CHUNK
    kv_w = N_KV_HEADS * HEAD_DIM
    first_tile = pl.program_id(1) == 0

    lane = lax.broadcasted_iota(jnp.int32, (BLOCK, LANES), 1)
    lo = lane < HEAD_DIM

    def chunk(i, carry):
        r0 = pl.multiple_of(i * CHUNK, CHUNK)
        seq_start = jnp.logical_and(first_tile, i == 0)

        @pl.when(seq_start)
        def _():
            kt_scr[:, :, 0:BLOCK] = jnp.zeros((N_KV_HEADS, 2 * HEAD_DIM, BLOCK), BF16)
            v_scr[:, 0:BLOCK, :] = jnp.zeros((N_KV_HEADS, BLOCK, 2 * HEAD_DIM), BF16)

        @pl.when(jnp.logical_not(seq_start))
        def _():
            kt_scr[:, :, 0:BLOCK] = kt_scr[:, :, CHUNK:CHUNK + BLOCK]
            v_scr[:, 0:BLOCK, :] = v_scr[:, CHUNK:CHUNK + BLOCK, :]

        x = x_ref[pl.ds(r0, CHUNK), :]
        xn = x * _rms_scale(x)
        h_kv = (xn * lnkv_ref[...]).astype(BF16)
        h_b = (xn * lnb_ref[...]).astype(BF16)

        kv = jnp.dot(h_kv, wkv_ref[...], preferred_element_type=F32)
        k_t = kv[:, 0:kv_w].T
        cos_t = cost_ref[i]
        sin_t = sint_ref[i]
        for g in range(N_KV_HEADS):
            kh = k_t[g * HEAD_DIM:(g + 1) * HEAD_DIM, :]
            kh = kh * lax.rsqrt(jnp.mean(kh * kh, axis=0, keepdims=True) + EPS) * kg_ref[...]
            r1 = kh[0:ROT_HALF, :]
            r2 = kh[ROT_HALF:ROT_DIM, :]
            kr = jnp.concatenate(
                [r1 * cos_t - r2 * sin_t, r2 * cos_t + r1 * sin_t, kh[ROT_DIM:, :]], axis=0)
            kt_scr[g, :, BLOCK:BLOCK + CHUNK] = jnp.concatenate([kr, kr], axis=0).astype(BF16)
        lo_c = lax.broadcasted_iota(jnp.int32, (CHUNK, LANES), 1) < HEAD_DIM
        for c in range(kv_w // LANES):
            vc = kv[:, kv_w + c * LANES:kv_w + (c + 1) * LANES]
            vr = pltpu.roll(vc, HEAD_DIM, axis=1)
            v_scr[2 * c, BLOCK:BLOCK + CHUNK, :] = jnp.where(lo_c, vc, vr).astype(BF16)
            v_scr[2 * c + 1, BLOCK:BLOCK + CHUNK, :] = jnp.where(lo_c, vr, vc).astype(BF16)

        qz = jnp.dot(h_b, winb_ref[...], preferred_element_type=F32)
        cq = cq_ref[pl.ds(r0, CHUNK), :]
        s1 = s1_ref[pl.ds(r0, CHUNK), :]
        s2 = s2_ref[pl.ds(r0, CHUNK), :]
        q_cols = []
        for c in range(d_model // LANES):
            qc = qz[:, c * LANES:(c + 1) * LANES]
            q2 = qc * qc
            ss_e = jnp.sum(jnp.where(lo_c, q2, 0.0), axis=-1, keepdims=True)
            ss_o = jnp.sum(jnp.where(lo_c, 0.0, q2), axis=-1, keepdims=True)
            ms = jnp.where(lo_c, ss_e, ss_o) * (1.0 / HEAD_DIM)
            qn = qc * lax.rsqrt(ms + EPS) * qg_ref[...]
            qr = (qn * cq + pltpu.roll(qn, LANES - ROT_HALF, axis=1) * s1
                  + pltpu.roll(qn, ROT_HALF, axis=1) * s2)
            q_cols.append(qr.astype(BF16))

        for qb in range(CHUNK // BLOCK):
            if qb == 0:
                bias = bias_ref[jnp.where(seq_start, 0, 1)]
            else:
                bias = bias_ref[1]
            rows = slice(qb * BLOCK, (qb + 1) * BLOCK)
            keys = slice(qb * BLOCK, qb * BLOCK + 2 * BLOCK)
            for c in range(d_model // LANES):
                g = c // (GROUP // 2)
                qc = q_cols[c][rows, :]
                kt = kt_scr[g, :, keys]
                vd = v_scr[g, keys, :]
                s_e = jnp.dot(jnp.where(lo, qc, jnp.zeros_like(qc)), kt, preferred_element_type=F32)
                s_o = jnp.dot(jnp.where(lo, jnp.zeros_like(qc), qc), kt, preferred_element_type=F32)
                p_e, r_e = _softmax_half(s_e, bias, sink_ref[2 * c])
                p_o, r_o = _softmax_half(s_o, bias, sink_ref[2 * c + 1])
                o_e = jnp.dot(p_e, vd, preferred_element_type=F32)
                o_o = jnp.dot(p_o, vd, preferred_element_type=F32)
                attn_scr[rows, c * LANES:(c + 1) * LANES] = jnp.where(lo, o_e * r_e, o_o * r_o)

        z = qz[:, d_model:2 * d_model]
        gated = (attn_scr[...] * (z * jax.nn.sigmoid(z))).astype(BF16)
        o_ref[pl.ds(r0, CHUNK), :] = x + jnp.dot(gated, woutb_ref[...], preferred_element_type=F32)
        return carry

    lax.fori_loop(0, n_chunks, chunk, 0)


def _rope_tables(seq):
    pos = jnp.arange(seq, dtype=F32)
    inv = ROPE_THETA ** (-jnp.arange(0, ROT_DIM, 2, dtype=F32) / ROT_DIM)
    ang = pos[:, None] * inv[None, :]
    return jnp.cos(ang), jnp.sin(ang)


def _layer_b(x, ln_kv, w_kv, k_norm, ln_b, w_in_b, q_norm, sinks, w_out_b):
    bsz, seq, d = x.shape
    kv2 = w_kv.shape[1]
    scale = 1.0 / math.sqrt(HEAD_DIM)

    cos, sin = _rope_tables(seq)
    cos_t = cos.T.reshape(ROT_HALF, seq // CHUNK, CHUNK).transpose(1, 0, 2)
    sin_t = sin.T.reshape(ROT_HALF, seq // CHUNK, CHUNK).transpose(1, 0, 2)
    dd = jnp.arange(LANES) % HEAD_DIM
    cos_l = cos[:, dd % ROT_HALF]
    sin_l = sin[:, dd % ROT_HALF]
    cq = jnp.where(dd < ROT_DIM, cos_l, 1.0) * scale
    s1 = jnp.where(dd < ROT_HALF, -sin_l, 0.0) * scale
    s2 = jnp.where((dd >= ROT_HALF) & (dd < ROT_DIM), sin_l, 0.0) * scale

    a_idx = jnp.arange(BLOCK)[:, None]
    c_idx = jnp.arange(2 * BLOCK)[None, :]
    rel_ok = (c_idx <= a_idx + BLOCK) & (c_idx > a_idx + BLOCK - WINDOW)
    big = jnp.float32(jnp.finfo(jnp.float32).max)
    bias_rest = jnp.where(rel_ok, big, NEG_INF).astype(F32)
    bias_first = jnp.where(rel_ok & (c_idx >= BLOCK), big, NEG_INF).astype(F32)
    bias = jnp.stack([bias_first, bias_rest])

    const2 = lambda b, j: (0, 0)
    const3 = lambda b, j: (0, 0, 0)
    single = pl.Buffered(1)
    return pl.pallas_call(
        _layer_b_kernel,
        out_shape=jax.ShapeDtypeStruct(x.shape, x.dtype),
        grid=(bsz, seq // SEQ_TILE),
        in_specs=[
            pl.BlockSpec((None, SEQ_TILE, d), lambda b, j: (b, j, 0)),
            pl.BlockSpec((1, d), const2),
            pl.BlockSpec((1, d), const2),
            pl.BlockSpec((d, kv2), const2, pipeline_mode=single),
            pl.BlockSpec((HEAD_DIM, 1), const2),
            pl.BlockSpec((SEQ_TILE // CHUNK, ROT_HALF, CHUNK), lambda b, j: (j, 0, 0)),
            pl.BlockSpec((SEQ_TILE // CHUNK, ROT_HALF, CHUNK), lambda b, j: (j, 0, 0)),
            pl.BlockSpec((d, 2 * d), const2, pipeline_mode=single),
            pl.BlockSpec((1, LANES), const2),
            pl.BlockSpec((SEQ_TILE, LANES), lambda b, j: (j, 0)),
            pl.BlockSpec((SEQ_TILE, LANES), lambda b, j: (j, 0)),
            pl.BlockSpec((SEQ_TILE, LANES), lambda b, j: (j, 0)),
            pl.BlockSpec(memory_space=pltpu.SMEM),
            pl.BlockSpec((2, BLOCK, 2 * BLOCK), const3),
            pl.BlockSpec((d, d), const2, pipeline_mode=single),
        ],
        out_specs=pl.BlockSpec((None, SEQ_TILE, d), lambda b, j: (b, j, 0)),
        scratch_shapes=[
            pltpu.VMEM((N_KV_HEADS, 2 * HEAD_DIM, BLOCK + CHUNK), BF16),
            pltpu.VMEM((N_KV_HEADS, BLOCK + CHUNK, 2 * HEAD_DIM), BF16),
            pltpu.VMEM((CHUNK, d), F32),
        ],
        compiler_params=pltpu.CompilerParams(
            dimension_semantics=("arbitrary", "arbitrary"),
            vmem_limit_bytes=VMEM_LIMIT_BYTES),
        name="layer_b_swa",
    )(x, ln_kv.reshape(1, d), ln_b.reshape(1, d), w_kv.astype(BF16), k_norm.reshape(HEAD_DIM, 1),
      cos_t, sin_t, w_in_b.astype(BF16), jnp.tile(q_norm, LANES // HEAD_DIM).reshape(1, LANES),
      cq, s1, s2, sinks.astype(F32), bias, w_out_b.astype(BF16))


def kernel(x, ln_a, w_in_a, conv_a, w_out_a, ln_kv, w_kv, k_norm, ln_b, w_in_b, q_norm, sinks, w_out_b):
    n_a = ln_a.shape[0]
    n_b = ln_b.shape[0]
    assert n_b == 1, "the shared K/V is recomputed inside the single attention layer"
    for i in range(n_a):
        x = _layer_a(x, ln_a[i], w_in_a[i], conv_a[i], w_out_a[i])
    for i in range(n_b):
        x = _layer_b(x, ln_kv, w_kv, k_norm, ln_b[i], w_in_b[i], q_norm[i], sinks[i], w_out_b[i])
    return x
```

```python
import functools
import math

import numpy as np

import jax
import jax.numpy as jnp
from jax import lax
from jax.experimental import pallas as pl
from jax.experimental.pallas import tpu as pltpu

F32 = jnp.float32
BF16 = jnp.bfloat16

HEAD_DIM = 64
N_KV_HEADS = 4
GROUP = 4
N_HEADS = N_KV_HEADS * GROUP
WINDOW = 128
BLOCK = 128
ROT_DIM = HEAD_DIM // 4
ROT_HALF = ROT_DIM // 2
ROPE_THETA = 500000.0
EPS = 1e-6
NEG_INF = -1e30
CONV_W = 3

LANES = 128
SUBLANES = 8
VMEM_LIMIT_BYTES = 56 * 1024 * 1024

SEQ_TILE = 512
CHUNK = 256

HALF_DIM = HEAD_DIM // 2
DIMS_A = np.concatenate([np.arange(0, ROT_HALF), np.arange(ROT_DIM, ROT_DIM + HALF_DIM - ROT_HALF)])
DIMS_B = np.concatenate([np.arange(ROT_HALF, ROT_DIM), np.arange(ROT_DIM + HALF_DIM - ROT_HALF, HEAD_DIM)])


def _rms_scale(x):
    return lax.rsqrt(jnp.mean(x * x, axis=-1, keepdims=True) + EPS)


def _layer_a_kernel(x_ref, g_ref, win_ref, cw_ref, wout_ref, o_ref, vbuf, *, width):
    w = width
    n_chunks = x_ref.shape[0] // CHUNK

    @pl.when(pl.program_id(1) == 0)
    def _():
        vbuf[0:SUBLANES, :] = jnp.zeros((SUBLANES, w), F32)

    def chunk(i, carry):
        r0 = pl.multiple_of(i * CHUNK, CHUNK)
        x = x_ref[pl.ds(r0, CHUNK), :]
        h = (x * _rms_scale(x) * g_ref[...]).astype(BF16)
        proj = jnp.dot(h, win_ref[...], preferred_element_type=F32)
        b_gate = proj[:, 0:w]
        c_gate = proj[:, w:2 * w]
        u = proj[:, 2 * w:3 * w]
        z = proj[:, 3 * w:4 * w]
        vbuf[SUBLANES:SUBLANES + CHUNK, :] = c_gate * u
        conv = (cw_ref[2:3, :] * vbuf[SUBLANES:SUBLANES + CHUNK, :]
                + cw_ref[1:2, :] * vbuf[SUBLANES - 1:SUBLANES - 1 + CHUNK, :]
                + cw_ref[0:1, :] * vbuf[SUBLANES - 2:SUBLANES - 2 + CHUNK, :])
        vbuf[0:SUBLANES, :] = vbuf[CHUNK:CHUNK + SUBLANES, :]
        y = (b_gate * conv * (z * jax.nn.sigmoid(z))).astype(BF16)
        o_ref[pl.ds(r0, CHUNK), :] = x + jnp.dot(y, wout_ref[...], preferred_element_type=F32)
        return carry

    lax.fori_loop(0, n_chunks, chunk, 0)


def _layer_a(x, ln_g, w_in, conv_w, w_out):
    bsz, seq, d = x.shape
    w = w_out.shape[0]
    const = lambda b, j: (0, 0)
    single = pl.Buffered(1)
    return pl.pallas_call(
        functools.partial(_layer_a_kernel, width=w),
        out_shape=jax.ShapeDtypeStruct(x.shape, x.dtype),
        grid=(bsz, seq // SEQ_TILE),
        in_specs=[
            pl.BlockSpec((None, SEQ_TILE, d), lambda b, j: (b, j, 0)),
            pl.BlockSpec((1, d), const),
            pl.BlockSpec((d, 4 * w), const, pipeline_mode=single),
            pl.BlockSpec((CONV_W, w), const),
            pl.BlockSpec((w, d), const, pipeline_mode=single),
        ],
        out_specs=pl.BlockSpec((None, SEQ_TILE, d), lambda b, j: (b, j, 0)),
        scratch_shapes=[pltpu.VMEM((CHUNK + SUBLANES, w), F32)],
        compiler_params=pltpu.CompilerParams(
            dimension_semantics=("arbitrary", "arbitrary"),
            vmem_limit_bytes=VMEM_LIMIT_BYTES),
        name="layer_a_shortconv",
    )(x, ln_g.reshape(1, d), w_in.astype(BF16), conv_w.reshape(CONV_W, w), w_out.astype(BF16))


def _layer_b_kernel(x_ref, lnkv_ref, lnb_ref, wkv_ref, kg_ref, cost_ref, sint_ref,
                    winb_ref, qga_ref, qgb_ref, cq_ref, sq_ref, sink_ref, bias_ref, woutb_ref,
                    o_ref, kt_scr, v_scr, attn_scr):
    d_model = x_ref.shape[1]
    n_chunks = x_ref.shape[0] // CHUNK
    kv_w = N_KV_HEADS * HEAD_DIM
    grp_w = GROUP * HEAD_DIM
    first_tile = pl.program_id(1) == 0

    lane_c = lax.broadcasted_iota(jnp.int32, (CHUNK, LANES), 1)
    lo_c = lane_c < HEAD_DIM
    quarter_c = [jnp.logical_and(lane_c >= hh * HALF_DIM, lane_c < (hh + 1) * HALF_DIM) for hh in range(GROUP)]
    lane_g = lax.broadcasted_iota(jnp.int32, (BLOCK, grp_w), 1) % LANES
    quarter_g = [jnp.logical_and(lane_g >= hh * HALF_DIM, lane_g < (hh + 1) * HALF_DIM) for hh in range(GROUP)]
    lo_b = lax.broadcasted_iota(jnp.int32, (BLOCK, LANES), 1) < HEAD_DIM

    def chunk(i, carry):
        r0 = pl.multiple_of(i * CHUNK, CHUNK)
        seq_start = jnp.logical_and(first_tile, i == 0)

        @pl.when(seq_start)
        def _():
            kt_scr[:, :, 0:BLOCK] = jnp.zeros((N_KV_HEADS, grp_w, BLOCK), BF16)
            v_scr[:, 0:BLOCK, :] = jnp.zeros((N_KV_HEADS, BLOCK, grp_w), BF16)

        @pl.when(jnp.logical_not(seq_start))
        def _():
            kt_scr[:, :, 0:BLOCK] = kt_scr[:, :, CHUNK:CHUNK + BLOCK]
            v_scr[:, 0:BLOCK, :] = v_scr[:, CHUNK:CHUNK + BLOCK, :]

        x = x_ref[pl.ds(r0, CHUNK), :]
        xn = x * _rms_scale(x)
        h_kv = (xn * lnkv_ref[...]).astype(BF16)
        h_b = (xn * lnb_ref[...]).astype(BF16)

        kv = jnp.dot(h_kv, wkv_ref[...], preferred_element_type=F32)
        k_t = kv[:, 0:kv_w].T
        cos_t = cost_ref[i]
        sin_t = sint_ref[i]
        for g in range(N_KV_HEADS):
            kh = k_t[g * HEAD_DIM:(g + 1) * HEAD_DIM, :]
            kh = kh * lax.rsqrt(jnp.mean(kh * kh, axis=0, keepdims=True) + EPS) * kg_ref[...]
            r1 = kh[0:ROT_HALF, :]
            r2 = kh[ROT_HALF:ROT_DIM, :]
            set_a = jnp.concatenate([r1 * cos_t - r2 * sin_t, kh[ROT_DIM:ROT_DIM + HALF_DIM - ROT_HALF, :]], axis=0)
            set_b = jnp.concatenate([r2 * cos_t + r1 * sin_t, kh[ROT_DIM + HALF_DIM - ROT_HALF:, :]], axis=0)
            kt_scr[g, :, BLOCK:BLOCK + CHUNK] = jnp.concatenate(
                [set_a] * GROUP + [set_b] * GROUP, axis=0).astype(BF16)
        ones = jnp.ones((CHUNK, LANES), BF16)
        for c in range(kv_w // LANES):
            vc = kv[:, kv_w + c * LANES:kv_w + (c + 1) * LANES]
            vr = pltpu.roll(vc, HEAD_DIM, axis=1)
            v_scr[2 * c, BLOCK:BLOCK + CHUNK, 0:LANES] = jnp.where(lo_c, vc, vr).astype(BF16)
            v_scr[2 * c + 1, BLOCK:BLOCK + CHUNK, 0:LANES] = jnp.where(lo_c, vr, vc).astype(BF16)
            v_scr[2 * c, BLOCK:BLOCK + CHUNK, LANES:2 * LANES] = ones
            v_scr[2 * c + 1, BLOCK:BLOCK + CHUNK, LANES:2 * LANES] = ones

        qz = jnp.dot(h_b, winb_ref[...], preferred_element_type=F32)
        cq = cq_ref[pl.ds(r0, CHUNK), :]
        sq = sq_ref[pl.ds(r0, CHUNK), :]
        q_groups = []
        for g in range(N_KV_HEADS):
            c0 = qz[:, g * grp_w:g * grp_w + LANES]
            c1 = qz[:, g * grp_w + LANES:(g + 1) * grp_w]
            t = c0 * c0 + c1 * c1
            ss = [jnp.sum(jnp.where(quarter_c[hh], t, 0.0), axis=-1, keepdims=True) for hh in range(GROUP)]
            ms = jnp.where(quarter_c[0], ss[0], jnp.where(quarter_c[1], ss[1], jnp.where(quarter_c[2], ss[2], ss[3])))
            rstd = lax.rsqrt(ms * (1.0 / HEAD_DIM) + EPS)
            n0 = c0 * rstd * qga_ref[...]
            n1 = c1 * rstd * qgb_ref[...]
            q_groups.append(jnp.concatenate([n0 * cq - n1 * sq, n1 * cq + n0 * sq], axis=1).astype(BF16))

        for qb in range(CHUNK // BLOCK):
            if qb == 0:
                bias = bias_ref[jnp.where(seq_start, 0, 1)]
            else:
                bias = bias_ref[1]
            rows = slice(qb * BLOCK, (qb + 1) * BLOCK)
            keys = slice(qb * BLOCK, qb * BLOCK + 2 * BLOCK)
            for g in range(N_KV_HEADS):
                qg = q_groups[g][rows, :]
                q4 = jnp.concatenate(
                    [jnp.where(quarter_g[hh], qg, jnp.zeros_like(qg)) for hh in range(GROUP)], axis=0)
                s = jnp.dot(q4, kt_scr[g, :, keys], preferred_element_type=F32)
                probs = []
                sink_terms = []
                for hh in range(GROUP):
                    sink = sink_ref[g * GROUP + hh]
                    sm = jnp.minimum(s[hh * BLOCK:(hh + 1) * BLOCK, :], bias)
                    m = jnp.maximum(jnp.max(sm, axis=-1, keepdims=True), sink)
                    probs.append(jnp.exp(sm - m).astype(BF16))
                    sink_terms.append(jnp.exp(sink - m))
                o4 = jnp.dot(jnp.concatenate(probs, axis=0), v_scr[g, keys, :],
                             preferred_element_type=F32)
                res = []
                for hh in range(GROUP):
                    blk = o4[hh * BLOCK:(hh + 1) * BLOCK, :]
                    res.append(blk[:, 0:LANES] * (1.0 / (blk[:, LANES:2 * LANES] + sink_terms[hh])))
                attn_scr[rows, g * grp_w:g * grp_w + LANES] = jnp.where(lo_b, res[0], res[1])
                attn_scr[rows, g * grp_w + LANES:(g + 1) * grp_w] = jnp.where(lo_b, res[2], res[3])

        z = qz[:, d_model:2 * d_model]
        gated = (attn_scr[...] * (z * jax.nn.sigmoid(z))).astype(BF16)
        o_ref[pl.ds(r0, CHUNK), :] = x + jnp.dot(gated, woutb_ref[...], preferred_element_type=F32)
        return carry

    lax.fori_loop(0, n_chunks, chunk, 0)


def _rope_tables(seq):
    pos = jnp.arange(seq, dtype=F32)
    inv = ROPE_THETA ** (-jnp.arange(0, ROT_DIM, 2, dtype=F32) / ROT_DIM)
    ang = pos[:, None] * inv[None, :]
    return jnp.cos(ang), jnp.sin(ang)


def _query_column_order():
    order = []
    for g in range(N_KV_HEADS):
        for dims in (DIMS_A, DIMS_B):
            for hh in range(GROUP):
                order.append((g * GROUP + hh) * HEAD_DIM + dims)
    return np.concatenate(order)


def _layer_b(x, ln_kv, w_kv, k_norm, ln_b, w_in_b, q_norm, sinks, w_out_b):
    bsz, seq, d = x.shape
    kv2 = w_kv.shape[1]
    grp_w = GROUP * HEAD_DIM
    scale = 1.0 / math.sqrt(HEAD_DIM)

    cos, sin = _rope_tables(seq)
    cos_t = cos.T.reshape(ROT_HALF, seq // CHUNK, CHUNK).transpose(1, 0, 2)
    sin_t = sin.T.reshape(ROT_HALF, seq // CHUNK, CHUNK).transpose(1, 0, 2)
    jj = jnp.arange(LANES) % HALF_DIM
    rot = jj < ROT_HALF
    cq = jnp.where(rot, cos[:, jj % ROT_HALF], 1.0) * scale
    sq = jnp.where(rot, sin[:, jj % ROT_HALF], 0.0) * scale
    qg_a = jnp.tile(q_norm[DIMS_A], GROUP).reshape(1, LANES)
    qg_b = jnp.tile(q_norm[DIMS_B], GROUP).reshape(1, LANES)
    w_qz = jnp.concatenate([w_in_b[:, :d][:, _query_column_order()], w_in_b[:, d:]], axis=1).astype(BF16)

    a_idx = jnp.arange(BLOCK)[:, None]
    c_idx = jnp.arange(2 * BLOCK)[None, :]
    rel_ok = (c_idx <= a_idx + BLOCK) & (c_idx > a_idx + BLOCK - WINDOW)
    big = jnp.float32(jnp.finfo(jnp.float32).max)
    bias_rest = jnp.where(rel_ok, big, NEG_INF).astype(F32)
    bias_first = jnp.where(rel_ok & (c_idx >= BLOCK), big, NEG_INF).astype(F32)
    bias = jnp.stack([bias_first, bias_rest])

    const2 = lambda b, j: (0, 0)
    const3 = lambda b, j: (0, 0, 0)
    single = pl.Buffered(1)
    return pl.pallas_call(
        _layer_b_kernel,
        out_shape=jax.ShapeDtypeStruct(x.shape, x.dtype),
        grid=(bsz, seq // SEQ_TILE),
        in_specs=[
            pl.BlockSpec((None, SEQ_TILE, d), lambda b, j: (b, j, 0)),
            pl.BlockSpec((1, d), const2),
            pl.BlockSpec((1, d), const2),
            pl.BlockSpec((d, kv2), const2, pipeline_mode=single),
            pl.BlockSpec((HEAD_DIM, 1), const2),
            pl.BlockSpec((SEQ_TILE // CHUNK, ROT_HALF, CHUNK), lambda b, j: (j, 0, 0)),
            pl.BlockSpec((SEQ_TILE // CHUNK, ROT_HALF, CHUNK), lambda b, j: (j, 0, 0)),
            pl.BlockSpec((d, 2 * d), const2, pipeline_mode=single),
            pl.BlockSpec((1, LANES), const2),
            pl.BlockSpec((1, LANES), const2),
            pl.BlockSpec((SEQ_TILE, LANES), lambda b, j: (j, 0)),
            pl.BlockSpec((SEQ_TILE, LANES), lambda b, j: (j, 0)),
            pl.BlockSpec(memory_space=pltpu.SMEM),
            pl.BlockSpec((2, BLOCK, 2 * BLOCK), const3),
            pl.BlockSpec((d, d), const2, pipeline_mode=single),
        ],
        out_specs=pl.BlockSpec((None, SEQ_TILE, d), lambda b, j: (b, j, 0)),
        scratch_shapes=[
            pltpu.VMEM((N_KV_HEADS, grp_w, BLOCK + CHUNK), BF16),
            pltpu.VMEM((N_KV_HEADS, BLOCK + CHUNK, grp_w), BF16),
            pltpu.VMEM((CHUNK, d), F32),
        ],
        compiler_params=pltpu.CompilerParams(
            dimension_semantics=("arbitrary", "arbitrary"),
            vmem_limit_bytes=VMEM_LIMIT_BYTES),
        name="layer_b_swa",
    )(x, ln_kv.reshape(1, d), ln_b.reshape(1, d), w_kv.astype(BF16), k_norm.reshape(HEAD_DIM, 1),
      cos_t, sin_t, w_qz, qg_a, qg_b, cq, sq, sinks.astype(F32), bias, w_out_b.astype(BF16))


def kernel(x, ln_a, w_in_a, conv_a, w_out_a, ln_kv, w_kv, k_norm, ln_b, w_in_b, q_norm, sinks, w_out_b):
    n_a = ln_a.shape[0]
    n_b = ln_b.shape[0]
    assert n_b == 1, "the shared K/V is recomputed inside the single attention layer"
    for i in range(n_a):
        x = _layer_a(x, ln_a[i], w_in_a[i], conv_a[i], w_out_a[i])
    for i in range(n_b):
        x = _layer_b(x, ln_kv, w_kv, k_norm, ln_b[i], w_in_b[i], q_norm[i], sinks[i], w_out_b[i])
    return x
```

```python
import functools
import math

import numpy as np

import jax
import jax.numpy as jnp
from jax import lax
from jax.experimental import pallas as pl
from jax.experimental.pallas import tpu as pltpu

F32 = jnp.float32
BF16 = jnp.bfloat16

HEAD_DIM = 64
N_KV_HEADS = 4
GROUP = 4
N_HEADS = N_KV_HEADS * GROUP
WINDOW = 128
BLOCK = 128
ROT_DIM = HEAD_DIM // 4
ROT_HALF = ROT_DIM // 2
ROPE_THETA = 500000.0
EPS = 1e-6
NEG_INF = -1e30
LOG2E = math.log2(math.e)
CONV_W = 3

LANES = 128
SUBLANES = 8
VMEM_LIMIT_BYTES = 56 * 1024 * 1024

SEQ_TILE = 1024
CHUNK = 512

HALF_DIM = HEAD_DIM // 2
DIMS_A = np.concatenate([np.arange(0, ROT_HALF), np.arange(ROT_DIM, ROT_DIM + HALF_DIM - ROT_HALF)])
DIMS_B = np.concatenate([np.arange(ROT_HALF, ROT_DIM), np.arange(ROT_DIM + HALF_DIM - ROT_HALF, HEAD_DIM)])


def _rms_scale(x):
    return lax.rsqrt(jnp.mean(x * x, axis=-1, keepdims=True) + EPS)


def _layer_a_kernel(x_ref, g_ref, win_ref, cw_ref, wout_ref, o_ref, vbuf, *, width):
    w = width
    n_chunks = x_ref.shape[0] // CHUNK

    @pl.when(pl.program_id(1) == 0)
    def _():
        vbuf[0:SUBLANES, :] = jnp.zeros((SUBLANES, w), F32)

    def chunk(i, carry):
        r0 = pl.multiple_of(i * CHUNK, CHUNK)
        x = x_ref[pl.ds(r0, CHUNK), :]
        h = (x * _rms_scale(x) * g_ref[...]).astype(BF16)
        proj = jnp.dot(h, win_ref[...], preferred_element_type=F32)
        b_gate = proj[:, 0:w]
        c_gate = proj[:, w:2 * w]
        u = proj[:, 2 * w:3 * w]
        z = proj[:, 3 * w:4 * w]
        vbuf[SUBLANES:SUBLANES + CHUNK, :] = c_gate * u
        conv = (cw_ref[2:3, :] * vbuf[SUBLANES:SUBLANES + CHUNK, :]
                + cw_ref[1:2, :] * vbuf[SUBLANES - 1:SUBLANES - 1 + CHUNK, :]
                + cw_ref[0:1, :] * vbuf[SUBLANES - 2:SUBLANES - 2 + CHUNK, :])
        vbuf[0:SUBLANES, :] = vbuf[CHUNK:CHUNK + SUBLANES, :]
        y = (b_gate * conv * (z * jax.nn.sigmoid(z))).astype(BF16)
        o_ref[pl.ds(r0, CHUNK), :] = x + jnp.dot(y, wout_ref[...], preferred_element_type=F32)
        return carry

    lax.fori_loop(0, n_chunks, chunk, 0)


def _layer_a(x, ln_g, w_in, conv_w, w_out):
    bsz, seq, d = x.shape
    w = w_out.shape[0]
    const = lambda b, j: (0, 0)
    single = pl.Buffered(1)
    return pl.pallas_call(
        functools.partial(_layer_a_kernel, width=w),
        out_shape=jax.ShapeDtypeStruct(x.shape, x.dtype),
        grid=(bsz, seq // SEQ_TILE),
        in_specs=[
            pl.BlockSpec((None, SEQ_TILE, d), lambda b, j: (b, j, 0)),
            pl.BlockSpec((1, d), const),
            pl.BlockSpec((d, 4 * w), const, pipeline_mode=single),
            pl.BlockSpec((CONV_W, w), const),
            pl.BlockSpec((w, d), const, pipeline_mode=single),
        ],
        out_specs=pl.BlockSpec((None, SEQ_TILE, d), lambda b, j: (b, j, 0)),
        scratch_shapes=[pltpu.VMEM((CHUNK + SUBLANES, w), F32)],
        compiler_params=pltpu.CompilerParams(
            dimension_semantics=("arbitrary", "arbitrary"),
            vmem_limit_bytes=VMEM_LIMIT_BYTES),
        name="layer_a_shortconv",
    )(x, ln_g.reshape(1, d), w_in.astype(BF16), conv_w.reshape(CONV_W, w), w_out.astype(BF16))


def _layer_b_kernel(x_ref, lnkv_ref, lnb_ref, wkv_ref, kg_ref, cost_ref, sint_ref,
                    winb_ref, qga_ref, qgb_ref, cq_ref, sq_ref, sink_ref, bias_ref, woutb_ref,
                    o_ref, kt_scr, v_scr, attn_scr):
    d_model = x_ref.shape[1]
    n_chunks = x_ref.shape[0] // CHUNK
    kv_w = N_KV_HEADS * HEAD_DIM
    grp_w = GROUP * HEAD_DIM
    first_tile = pl.program_id(1) == 0

    lane_c = lax.broadcasted_iota(jnp.int32, (CHUNK, LANES), 1)
    lo_c = lane_c < HEAD_DIM
    quarter_c = [jnp.logical_and(lane_c >= hh * HALF_DIM, lane_c < (hh + 1) * HALF_DIM) for hh in range(GROUP)]
    lane_g = lax.broadcasted_iota(jnp.int32, (BLOCK, grp_w), 1) % LANES
    quarter_g = [jnp.logical_and(lane_g >= hh * HALF_DIM, lane_g < (hh + 1) * HALF_DIM) for hh in range(GROUP)]
    lo_b = lax.broadcasted_iota(jnp.int32, (BLOCK, LANES), 1) < HEAD_DIM

    def chunk(i, carry):
        r0 = pl.multiple_of(i * CHUNK, CHUNK)
        seq_start = jnp.logical_and(first_tile, i == 0)

        @pl.when(seq_start)
        def _():
            kt_scr[:, :, 0:BLOCK] = jnp.zeros((N_KV_HEADS, grp_w, BLOCK), BF16)
            v_scr[:, 0:BLOCK, :] = jnp.zeros((N_KV_HEADS, BLOCK, grp_w), BF16)

        @pl.when(jnp.logical_not(seq_start))
        def _():
            kt_scr[:, :, 0:BLOCK] = kt_scr[:, :, CHUNK:CHUNK + BLOCK]
            v_scr[:, 0:BLOCK, :] = v_scr[:, CHUNK:CHUNK + BLOCK, :]

        x = x_ref[pl.ds(r0, CHUNK), :]
        xn = x * _rms_scale(x)
        h_kv = (xn * lnkv_ref[...]).astype(BF16)
        h_b = (xn * lnb_ref[...]).astype(BF16)

        kv = jnp.dot(h_kv, wkv_ref[...], preferred_element_type=F32)
        k_t = kv[:, 0:kv_w].T
        cos_t = cost_ref[i]
        sin_t = sint_ref[i]
        for g in range(N_KV_HEADS):
            kh = k_t[g * HEAD_DIM:(g + 1) * HEAD_DIM, :]
            kh = kh * lax.rsqrt(jnp.mean(kh * kh, axis=0, keepdims=True) + EPS) * kg_ref[...]
            r1 = kh[0:ROT_HALF, :]
            r2 = kh[ROT_HALF:ROT_DIM, :]
            set_a = jnp.concatenate([r1 * cos_t - r2 * sin_t, kh[ROT_DIM:ROT_DIM + HALF_DIM - ROT_HALF, :]], axis=0)
            set_b = jnp.concatenate([r2 * cos_t + r1 * sin_t, kh[ROT_DIM + HALF_DIM - ROT_HALF:, :]], axis=0)
            kt_scr[g, :, BLOCK:BLOCK + CHUNK] = jnp.concatenate(
                [set_a] * GROUP + [set_b] * GROUP, axis=0).astype(BF16)
        ones = jnp.ones((CHUNK, LANES), BF16)
        for c in range(kv_w // LANES):
            vc = kv[:, kv_w + c * LANES:kv_w + (c + 1) * LANES]
            vr = pltpu.roll(vc, HEAD_DIM, axis=1)
            v_scr[2 * c, BLOCK:BLOCK + CHUNK, 0:LANES] = jnp.where(lo_c, vc, vr).astype(BF16)
            v_scr[2 * c + 1, BLOCK:BLOCK + CHUNK, 0:LANES] = jnp.where(lo_c, vr, vc).astype(BF16)
            v_scr[2 * c, BLOCK:BLOCK + CHUNK, LANES:2 * LANES] = ones
            v_scr[2 * c + 1, BLOCK:BLOCK + CHUNK, LANES:2 * LANES] = ones

        qz = jnp.dot(h_b, winb_ref[...], preferred_element_type=F32)
        cq = cq_ref[pl.ds(r0, CHUNK), :]
        sq = sq_ref[pl.ds(r0, CHUNK), :]
        q_groups = []
        for g in range(N_KV_HEADS):
            c0 = qz[:, g * grp_w:g * grp_w + LANES]
            c1 = qz[:, g * grp_w + LANES:(g + 1) * grp_w]
            t = c0 * c0 + c1 * c1
            ss = [jnp.sum(jnp.where(quarter_c[hh], t, 0.0), axis=-1, keepdims=True) for hh in range(GROUP)]
            ms = jnp.where(quarter_c[0], ss[0], jnp.where(quarter_c[1], ss[1], jnp.where(quarter_c[2], ss[2], ss[3])))
            rstd = lax.rsqrt(ms * (1.0 / HEAD_DIM) + EPS)
            n0 = c0 * rstd * qga_ref[...]
            n1 = c1 * rstd * qgb_ref[...]
            q_groups.append(jnp.concatenate([n0 * cq - n1 * sq, n1 * cq + n0 * sq], axis=1).astype(BF16))

        for qb in range(CHUNK // BLOCK):
            if qb == 0:
                bias = bias_ref[jnp.where(seq_start, 0, 1)]
            else:
                bias = bias_ref[1]
            rows = slice(qb * BLOCK, (qb + 1) * BLOCK)
            keys = slice(qb * BLOCK, qb * BLOCK + 2 * BLOCK)
            for g in range(N_KV_HEADS):
                qg = q_groups[g][rows, :]
                q4 = jnp.concatenate(
                    [jnp.where(quarter_g[hh], qg, jnp.zeros_like(qg)) for hh in range(GROUP)], axis=0)
                s = jnp.dot(q4, kt_scr[g, :, keys], preferred_element_type=F32)
                probs = []
                sink_terms = []
                for hh in range(GROUP):
                    sink = sink_ref[g * GROUP + hh] * LOG2E
                    sm = jnp.minimum(s[hh * BLOCK:(hh + 1) * BLOCK, :], bias)
                    m = jnp.max(sm, axis=-1, keepdims=True)
                    probs.append(jnp.exp2(sm - m).astype(BF16))
                    sink_terms.append(jnp.exp2(sink - m))
                o4 = jnp.dot(jnp.concatenate(probs, axis=0), v_scr[g, keys, :],
                             preferred_element_type=F32)
                res = []
                for hh in range(GROUP):
                    blk = o4[hh * BLOCK:(hh + 1) * BLOCK, :]
                    res.append(blk[:, 0:LANES] * (1.0 / (blk[:, LANES:2 * LANES] + sink_terms[hh])))
                attn_scr[rows, g * grp_w:g * grp_w + LANES] = jnp.where(lo_b, res[0], res[1])
                attn_scr[rows, g * grp_w + LANES:(g + 1) * grp_w] = jnp.where(lo_b, res[2], res[3])

        z = qz[:, d_model:2 * d_model]
        gated = (attn_scr[...] * (z * jax.nn.sigmoid(z))).astype(BF16)
        o_ref[pl.ds(r0, CHUNK), :] = x + jnp.dot(gated, woutb_ref[...], preferred_element_type=F32)
        return carry

    lax.fori_loop(0, n_chunks, chunk, 0)


def _rope_tables(seq):
    pos = jnp.arange(seq, dtype=F32)
    inv = ROPE_THETA ** (-jnp.arange(0, ROT_DIM, 2, dtype=F32) / ROT_DIM)
    ang = pos[:, None] * inv[None, :]
    return jnp.cos(ang), jnp.sin(ang)


def _query_column_order():
    order = []
    for g in range(N_KV_HEADS):
        for dims in (DIMS_A, DIMS_B):
            for hh in range(GROUP):
                order.append((g * GROUP + hh) * HEAD_DIM + dims)
    return np.concatenate(order)


def _layer_b(x, ln_kv, w_kv, k_norm, ln_b, w_in_b, q_norm, sinks, w_out_b):
    bsz, seq, d = x.shape
    kv2 = w_kv.shape[1]
    grp_w = GROUP * HEAD_DIM
    scale = LOG2E / math.sqrt(HEAD_DIM)

    cos, sin = _rope_tables(seq)
    cos_t = cos.T.reshape(ROT_HALF, seq // CHUNK, CHUNK).transpose(1, 0, 2)
    sin_t = sin.T.reshape(ROT_HALF, seq // CHUNK, CHUNK).transpose(1, 0, 2)
    jj = jnp.arange(LANES) % HALF_DIM
    rot = jj < ROT_HALF
    cq = jnp.where(rot, cos[:, jj % ROT_HALF], 1.0) * scale
    sq = jnp.where(rot, sin[:, jj % ROT_HALF], 0.0) * scale
    qg_a = jnp.tile(q_norm[DIMS_A], GROUP).reshape(1, LANES)
    qg_b = jnp.tile(q_norm[DIMS_B], GROUP).reshape(1, LANES)
    w_qz = jnp.concatenate([w_in_b[:, :d][:, _query_column_order()], w_in_b[:, d:]], axis=1).astype(BF16)

    a_idx = jnp.arange(BLOCK)[:, None]
    c_idx = jnp.arange(2 * BLOCK)[None, :]
    rel_ok = (c_idx <= a_idx + BLOCK) & (c_idx > a_idx + BLOCK - WINDOW)
    big = jnp.float32(jnp.finfo(jnp.float32).max)
    bias_rest = jnp.where(rel_ok, big, NEG_INF).astype(F32)
    bias_first = jnp.where(rel_ok & (c_idx >= BLOCK), big, NEG_INF).astype(F32)
    bias = jnp.stack([bias_first, bias_rest])

    const2 = lambda b, j: (0, 0)
    const3 = lambda b, j: (0, 0, 0)
    single = pl.Buffered(1)
    return pl.pallas_call(
        _layer_b_kernel,
        out_shape=jax.ShapeDtypeStruct(x.shape, x.dtype),
        grid=(bsz, seq // SEQ_TILE),
        in_specs=[
            pl.BlockSpec((None, SEQ_TILE, d), lambda b, j: (b, j, 0)),
            pl.BlockSpec((1, d), const2),
            pl.BlockSpec((1, d), const2),
            pl.BlockSpec((d, kv2), const2, pipeline_mode=single),
            pl.BlockSpec((HEAD_DIM, 1), const2),
            pl.BlockSpec((SEQ_TILE // CHUNK, ROT_HALF, CHUNK), lambda b, j: (j, 0, 0)),
            pl.BlockSpec((SEQ_TILE // CHUNK, ROT_HALF, CHUNK), lambda b, j: (j, 0, 0)),
            pl.BlockSpec((d, 2 * d), const2, pipeline_mode=single),
            pl.BlockSpec((1, LANES), const2),
            pl.BlockSpec((1, LANES), const2),
            pl.BlockSpec((SEQ_TILE, LANES), lambda b, j: (j, 0)),
            pl.BlockSpec((SEQ_TILE, LANES), lambda b, j: (j, 0)),
            pl.BlockSpec(memory_space=pltpu.SMEM),
            pl.BlockSpec((2, BLOCK, 2 * BLOCK), const3),
            pl.BlockSpec((d, d), const2, pipeline_mode=single),
        ],
        out_specs=pl.BlockSpec((None, SEQ_TILE, d), lambda b, j: (b, j, 0)),
        scratch_shapes=[
            pltpu.VMEM((N_KV_HEADS, grp_w, BLOCK + CHUNK), BF16),
            pltpu.VMEM((N_KV_HEADS, BLOCK + CHUNK, grp_w), BF16),
            pltpu.VMEM((CHUNK, d), F32),
        ],
        compiler_params=pltpu.CompilerParams(
            dimension_semantics=("arbitrary", "arbitrary"),
            vmem_limit_bytes=VMEM_LIMIT_BYTES),
        name="layer_b_swa",
    )(x, ln_kv.reshape(1, d), ln_b.reshape(1, d), w_kv.astype(BF16), k_norm.reshape(HEAD_DIM, 1),
      cos_t, sin_t, w_qz, qg_a, qg_b, cq, sq, sinks.astype(F32), bias, w_out_b.astype(BF16))


def kernel(x, ln_a, w_in_a, conv_a, w_out_a, ln_kv, w_kv, k_norm, ln_b, w_in_b, q_norm, sinks, w_out_b):
    n_a = ln_a.shape[0]
    n_b = ln_b.shape[0]
    assert n_b == 1, "the shared K/V is recomputed inside the single attention layer"
    for i in range(n_a):
        x = _layer_a(x, ln_a[i], w_in_a[i], conv_a[i], w_out_a[i])
    for i in range(n_b):
        x = _layer_b(x, ln_kv, w_kv, k_norm, ln_b[i], w_in_b[i], q_norm[i], sinks[i], w_out_b[i])
    return x
```

```python
import functools
import math

import numpy as np

import jax
import jax.numpy as jnp
from jax import lax
from jax.experimental import pallas as pl
from jax.experimental.pallas import tpu as pltpu

F32 = jnp.float32
BF16 = jnp.bfloat16

HEAD_DIM = 64
N_KV_HEADS = 4
GROUP = 4
N_HEADS = N_KV_HEADS * GROUP
WINDOW = 128
BLOCK = 128
ROT_DIM = HEAD_DIM // 4
ROT_HALF = ROT_DIM // 2
ROPE_THETA = 500000.0
EPS = 1e-6
NEG_INF = -1e30
LOG2E = math.log2(math.e)
CONV_W = 3

LANES = 128
SUBLANES = 8
VMEM_LIMIT_BYTES = 56 * 1024 * 1024

SEQ_TILE = 1024
CHUNK = 512

HALF_DIM = HEAD_DIM // 2
DIMS_A = np.concatenate([np.arange(0, ROT_HALF), np.arange(ROT_DIM, ROT_DIM + HALF_DIM - ROT_HALF)])
DIMS_B = np.concatenate([np.arange(ROT_HALF, ROT_DIM), np.arange(ROT_DIM + HALF_DIM - ROT_HALF, HEAD_DIM)])


def _rms_scale(x):
    return lax.rsqrt(jnp.mean(x * x, axis=-1, keepdims=True) + EPS)


def _layer_a_kernel(x_ref, g_ref, win_ref, cw_ref, wout_ref, o_ref, vbuf, *, width):
    w = width
    n_chunks = x_ref.shape[0] // CHUNK

    @pl.when(pl.program_id(1) == 0)
    def _():
        vbuf[0:SUBLANES, :] = jnp.zeros((SUBLANES, w), F32)

    def chunk(i, carry):
        r0 = pl.multiple_of(i * CHUNK, CHUNK)
        x = x_ref[pl.ds(r0, CHUNK), :]
        h = (x * _rms_scale(x) * g_ref[...]).astype(BF16)
        proj = jnp.dot(h, win_ref[...], preferred_element_type=F32)
        b_gate = proj[:, 0:w]
        c_gate = proj[:, w:2 * w]
        u = proj[:, 2 * w:3 * w]
        z = proj[:, 3 * w:4 * w]
        vbuf[SUBLANES:SUBLANES + CHUNK, :] = c_gate * u
        conv = (cw_ref[2:3, :] * vbuf[SUBLANES:SUBLANES + CHUNK, :]
                + cw_ref[1:2, :] * vbuf[SUBLANES - 1:SUBLANES - 1 + CHUNK, :]
                + cw_ref[0:1, :] * vbuf[SUBLANES - 2:SUBLANES - 2 + CHUNK, :])
        vbuf[0:SUBLANES, :] = vbuf[CHUNK:CHUNK + SUBLANES, :]
        y = (b_gate * conv * (z * jax.nn.sigmoid(z))).astype(BF16)
        o_ref[pl.ds(r0, CHUNK), :] = x + jnp.dot(y, wout_ref[...], preferred_element_type=F32)
        return carry

    lax.fori_loop(0, n_chunks, chunk, 0)


def _layer_a(x, ln_g, w_in, conv_w, w_out):
    bsz, seq, d = x.shape
    w = w_out.shape[0]
    const = lambda b, j: (0, 0)
    single = pl.Buffered(1)
    return pl.pallas_call(
        functools.partial(_layer_a_kernel, width=w),
        out_shape=jax.ShapeDtypeStruct(x.shape, x.dtype),
        grid=(bsz, seq // SEQ_TILE),
        in_specs=[
            pl.BlockSpec((None, SEQ_TILE, d), lambda b, j: (b, j, 0)),
            pl.BlockSpec((1, d), const),
            pl.BlockSpec((d, 4 * w), const, pipeline_mode=single),
            pl.BlockSpec((CONV_W, w), const),
            pl.BlockSpec((w, d), const, pipeline_mode=single),
        ],
        out_specs=pl.BlockSpec((None, SEQ_TILE, d), lambda b, j: (b, j, 0)),
        scratch_shapes=[pltpu.VMEM((CHUNK + SUBLANES, w), F32)],
        compiler_params=pltpu.CompilerParams(
            dimension_semantics=("arbitrary", "arbitrary"),
            vmem_limit_bytes=VMEM_LIMIT_BYTES),
        name="layer_a_shortconv",
    )(x, ln_g.reshape(1, d), w_in.astype(BF16), conv_w.reshape(CONV_W, w), w_out.astype(BF16))


def _layer_b_kernel(x_ref, lnkv_ref, lnb_ref, wkv_ref, kg_ref, cost_ref, sint_ref,
                    winb_ref, qga_ref, qgb_ref, cq_ref, sq_ref, sink_ref, bias_ref, woutb_ref,
                    o_ref, kt_scr, v_scr, attn_scr):
    d_model = x_ref.shape[1]
    n_chunks = x_ref.shape[0] // CHUNK
    kv_w = N_KV_HEADS * HEAD_DIM
    grp_w = GROUP * HEAD_DIM
    first_tile = pl.program_id(1) == 0

    lane_c = lax.broadcasted_iota(jnp.int32, (CHUNK, LANES), 1)
    lo_c = lane_c < HEAD_DIM
    quarter_c = [jnp.logical_and(lane_c >= hh * HALF_DIM, lane_c < (hh + 1) * HALF_DIM) for hh in range(GROUP)]
    lane_g = lax.broadcasted_iota(jnp.int32, (BLOCK, grp_w), 1) % LANES
    quarter_g = [jnp.logical_and(lane_g >= hh * HALF_DIM, lane_g < (hh + 1) * HALF_DIM) for hh in range(GROUP)]
    lo_b = lax.broadcasted_iota(jnp.int32, (BLOCK, LANES), 1) < HEAD_DIM

    def chunk(i, carry):
        r0 = pl.multiple_of(i * CHUNK, CHUNK)
        seq_start = jnp.logical_and(first_tile, i == 0)

        @pl.when(seq_start)
        def _():
            kt_scr[:, :, 0:BLOCK] = jnp.zeros((N_KV_HEADS, grp_w, BLOCK), BF16)
            v_scr[:, 0:BLOCK, :] = jnp.zeros((N_KV_HEADS, BLOCK, grp_w), BF16)

        @pl.when(jnp.logical_not(seq_start))
        def _():
            kt_scr[:, :, 0:BLOCK] = kt_scr[:, :, CHUNK:CHUNK + BLOCK]
            v_scr[:, 0:BLOCK, :] = v_scr[:, CHUNK:CHUNK + BLOCK, :]

        x = x_ref[pl.ds(r0, CHUNK), :]
        xn = x * _rms_scale(x)
        h_kv = (xn * lnkv_ref[...]).astype(BF16)
        h_b = (xn * lnb_ref[...]).astype(BF16)

        kv = jnp.dot(h_kv, wkv_ref[...], preferred_element_type=F32)
        k_t = kv[:, 0:kv_w].T
        cos_t = cost_ref[i]
        sin_t = sint_ref[i]
        for g in range(N_KV_HEADS):
            kh = k_t[g * HEAD_DIM:(g + 1) * HEAD_DIM, :]
            kh = kh * lax.rsqrt(jnp.mean(kh * kh, axis=0, keepdims=True) + EPS) * kg_ref[...]
            r1 = kh[0:ROT_HALF, :]
            r2 = kh[ROT_HALF:ROT_DIM, :]
            set_a = jnp.concatenate([r1 * cos_t - r2 * sin_t, kh[ROT_DIM:ROT_DIM + HALF_DIM - ROT_HALF, :]], axis=0)
            set_b = jnp.concatenate([r2 * cos_t + r1 * sin_t, kh[ROT_DIM + HALF_DIM - ROT_HALF:, :]], axis=0)
            kt_scr[g, :, BLOCK:BLOCK + CHUNK] = jnp.concatenate(
                [set_a] * GROUP + [set_b] * GROUP, axis=0).astype(BF16)
        ones = jnp.ones((CHUNK, LANES), BF16)
        for c in range(kv_w // LANES):
            vc = kv[:, kv_w + c * LANES:kv_w + (c + 1) * LANES]
            vr = pltpu.roll(vc, HEAD_DIM, axis=1)
            v_scr[2 * c, BLOCK:BLOCK + CHUNK, 0:LANES] = jnp.where(lo_c, vc, vr).astype(BF16)
            v_scr[2 * c + 1, BLOCK:BLOCK + CHUNK, 0:LANES] = jnp.where(lo_c, vr, vc).astype(BF16)
            v_scr[2 * c, BLOCK:BLOCK + CHUNK, LANES:2 * LANES] = ones
            v_scr[2 * c + 1, BLOCK:BLOCK + CHUNK, LANES:2 * LANES] = ones

        qz = jnp.dot(h_b, winb_ref[...], preferred_element_type=F32)
        cq = cq_ref[pl.ds(r0, CHUNK), :]
        sq = sq_ref[pl.ds(r0, CHUNK), :]
        q_groups = []
        for g in range(N_KV_HEADS):
            c0 = qz[:, g * grp_w:g * grp_w + LANES]
            c1 = qz[:, g * grp_w + LANES:(g + 1) * grp_w]
            t = c0 * c0 + c1 * c1
            ss = [jnp.sum(jnp.where(quarter_c[hh], t, 0.0), axis=-1, keepdims=True) for hh in range(GROUP)]
            ms = jnp.where(quarter_c[0], ss[0], jnp.where(quarter_c[1], ss[1], jnp.where(quarter_c[2], ss[2], ss[3])))
            rstd = lax.rsqrt(ms * (1.0 / HEAD_DIM) + EPS)
            n0 = c0 * rstd * qga_ref[...]
            n1 = c1 * rstd * qgb_ref[...]
            q_groups.append(jnp.concatenate([n0 * cq - n1 * sq, n1 * cq + n0 * sq], axis=1).astype(BF16))

        for qb in range(CHUNK // BLOCK):
            if qb == 0:
                bias = bias_ref[jnp.where(seq_start, 0, 1)]
            else:
                bias = bias_ref[1]
            rows = slice(qb * BLOCK, (qb + 1) * BLOCK)
            keys = slice(qb * BLOCK, qb * BLOCK + 2 * BLOCK)
            for g in range(N_KV_HEADS):
                qg = q_groups[g][rows, :]
                q4 = jnp.concatenate(
                    [jnp.where(quarter_g[hh], qg, jnp.zeros_like(qg)) for hh in range(GROUP)], axis=0)
                s = jnp.dot(q4, kt_scr[g, :, keys], preferred_element_type=F32)
                probs = []
                sink_terms = []
                for hh in range(GROUP):
                    sink = sink_ref[g * GROUP + hh] * LOG2E
                    sm = jnp.minimum(s[hh * BLOCK:(hh + 1) * BLOCK, :], bias)
                    m = jnp.max(sm, axis=-1, keepdims=True)
                    probs.append(jnp.exp2(sm - m).astype(BF16))
                    sink_terms.append(jnp.exp2(sink - m))
                o4 = jnp.dot(jnp.concatenate(probs, axis=0), v_scr[g, keys, :],
                             preferred_element_type=F32)
                res = []
                for hh in range(GROUP):
                    blk = o4[hh * BLOCK:(hh + 1) * BLOCK, :]
                    res.append(blk[:, 0:LANES] * (1.0 / (blk[:, LANES:2 * LANES] + sink_terms[hh])))
                attn_scr[rows, g * grp_w:g * grp_w + LANES] = jnp.where(lo_b, res[0], res[1])
                attn_scr[rows, g * grp_w + LANES:(g + 1) * grp_w] = jnp.where(lo_b, res[2], res[3])

        z = qz[:, d_model:2 * d_model]
        gated = (attn_scr[...] * (z * jax.nn.sigmoid(z))).astype(BF16)
        o_ref[pl.ds(r0, CHUNK), :] = x + jnp.dot(gated, woutb_ref[...], preferred_element_type=F32)
        return carry

    lax.fori_loop(0, n_chunks, chunk, 0)


def _rope_tables(seq):
    pos = np.arange(seq, dtype=np.float32)
    inv = np.float32(ROPE_THETA) ** (-np.arange(0, ROT_DIM, 2, dtype=np.float32) / np.float32(ROT_DIM))
    ang = pos[:, None] * inv[None, :]
    return np.cos(ang), np.sin(ang)


def _query_column_order():
    order = []
    for g in range(N_KV_HEADS):
        for dims in (DIMS_A, DIMS_B):
            for hh in range(GROUP):
                order.append((g * GROUP + hh) * HEAD_DIM + dims)
    return np.concatenate(order)


def _layer_b(x, ln_kv, w_kv, k_norm, ln_b, w_in_b, q_norm, sinks, w_out_b):
    bsz, seq, d = x.shape
    kv2 = w_kv.shape[1]
    grp_w = GROUP * HEAD_DIM
    scale = LOG2E / math.sqrt(HEAD_DIM)

    cos, sin = _rope_tables(seq)
    cos_t = np.ascontiguousarray(cos.T.reshape(ROT_HALF, seq // CHUNK, CHUNK).transpose(1, 0, 2))
    sin_t = np.ascontiguousarray(sin.T.reshape(ROT_HALF, seq // CHUNK, CHUNK).transpose(1, 0, 2))
    jj = np.arange(LANES) % HALF_DIM
    rot = jj < ROT_HALF
    cq = (np.where(rot, cos[:, jj % ROT_HALF], 1.0) * scale).astype(np.float32)
    sq = (np.where(rot, sin[:, jj % ROT_HALF], 0.0) * scale).astype(np.float32)
    qg_a = jnp.tile(q_norm[DIMS_A], GROUP).reshape(1, LANES)
    qg_b = jnp.tile(q_norm[DIMS_B], GROUP).reshape(1, LANES)
    w_qz = jnp.concatenate([w_in_b[:, :d][:, _query_column_order()], w_in_b[:, d:]], axis=1).astype(BF16)

    a_idx = np.arange(BLOCK)[:, None]
    c_idx = np.arange(2 * BLOCK)[None, :]
    rel_ok = (c_idx <= a_idx + BLOCK) & (c_idx > a_idx + BLOCK - WINDOW)
    big = np.finfo(np.float32).max
    bias_rest = np.where(rel_ok, big, NEG_INF).astype(np.float32)
    bias_first = np.where(rel_ok & (c_idx >= BLOCK), big, NEG_INF).astype(np.float32)
    bias = np.stack([bias_first, bias_rest])

    const2 = lambda b, j: (0, 0)
    const3 = lambda b, j: (0, 0, 0)
    single = pl.Buffered(1)
    return pl.pallas_call(
        _layer_b_kernel,
        out_shape=jax.ShapeDtypeStruct(x.shape, x.dtype),
        grid=(bsz, seq // SEQ_TILE),
        in_specs=[
            pl.BlockSpec((None, SEQ_TILE, d), lambda b, j: (b, j, 0)),
            pl.BlockSpec((1, d), const2),
            pl.BlockSpec((1, d), const2),
            pl.BlockSpec((d, kv2), const2, pipeline_mode=single),
            pl.BlockSpec((HEAD_DIM, 1), const2),
            pl.BlockSpec((SEQ_TILE // CHUNK, ROT_HALF, CHUNK), lambda b, j: (j, 0, 0)),
            pl.BlockSpec((SEQ_TILE // CHUNK, ROT_HALF, CHUNK), lambda b, j: (j, 0, 0)),
            pl.BlockSpec((d, 2 * d), const2, pipeline_mode=single),
            pl.BlockSpec((1, LANES), const2),
            pl.BlockSpec((1, LANES), const2),
            pl.BlockSpec((SEQ_TILE, LANES), lambda b, j: (j, 0)),
            pl.BlockSpec((SEQ_TILE, LANES), lambda b, j: (j, 0)),
            pl.BlockSpec(memory_space=pltpu.SMEM),
            pl.BlockSpec((2, BLOCK, 2 * BLOCK), const3),
            pl.BlockSpec((d, d), const2, pipeline_mode=single),
        ],
        out_specs=pl.BlockSpec((None, SEQ_TILE, d), lambda b, j: (b, j, 0)),
        scratch_shapes=[
            pltpu.VMEM((N_KV_HEADS, grp_w, BLOCK + CHUNK), BF16),
            pltpu.VMEM((N_KV_HEADS, BLOCK + CHUNK, grp_w), BF16),
            pltpu.VMEM((CHUNK, d), F32),
        ],
        compiler_params=pltpu.CompilerParams(
            dimension_semantics=("arbitrary", "arbitrary"),
            vmem_limit_bytes=VMEM_LIMIT_BYTES),
        name="layer_b_swa",
    )(x, ln_kv.reshape(1, d), ln_b.reshape(1, d), w_kv.astype(BF16), k_norm.reshape(HEAD_DIM, 1),
      cos_t, sin_t, w_qz, qg_a, qg_b, cq, sq, sinks.astype(F32), bias, w_out_b.astype(BF16))


def kernel(x, ln_a, w_in_a, conv_a, w_out_a, ln_kv, w_kv, k_norm, ln_b, w_in_b, q_norm, sinks, w_out_b):
    n_a = ln_a.shape[0]
    n_b = ln_b.shape[0]
    assert n_b == 1, "the shared K/V is recomputed inside the single attention layer"
    for i in range(n_a):
        x = _layer_a(x, ln_a[i], w_in_a[i], conv_a[i], w_out_a[i])
    for i in range(n_b):
        x = _layer_b(x, ln_kv, w_kv, k_norm, ln_b[i], w_in_b[i], q_norm[i], sinks[i], w_out_b[i])
    return x
```

```python
import functools
import math

import numpy as np

import jax
import jax.numpy as jnp
from jax import lax
from jax.experimental import pallas as pl
from jax.experimental.pallas import tpu as pltpu

F32 = jnp.float32
BF16 = jnp.bfloat16

HEAD_DIM = 64
N_KV_HEADS = 4
GROUP = 4
N_HEADS = N_KV_HEADS * GROUP
WINDOW = 128
BLOCK = 128
ROT_DIM = HEAD_DIM // 4
ROT_HALF = ROT_DIM // 2
ROPE_THETA = 500000.0
EPS = 1e-6
NEG_INF = -1e30
LOG2E = math.log2(math.e)
CONV_W = 3

LANES = 128
SUBLANES = 8
VMEM_LIMIT_BYTES = 56 * 1024 * 1024

SEQ_TILE = 1024
CHUNK = 512
SCORE_LOOKAHEAD = 2

HALF_DIM = HEAD_DIM // 2
DIMS_A = np.concatenate([np.arange(0, ROT_HALF), np.arange(ROT_DIM, ROT_DIM + HALF_DIM - ROT_HALF)])
DIMS_B = np.concatenate([np.arange(ROT_HALF, ROT_DIM), np.arange(ROT_DIM + HALF_DIM - ROT_HALF, HEAD_DIM)])


def _rms_scale(x):
    return lax.rsqrt(jnp.mean(x * x, axis=-1, keepdims=True) + EPS)


def _layer_a_kernel(x_ref, g_ref, win_ref, cw_ref, wout_ref, o_ref, vbuf, *, width):
    w = width
    tile = x_ref.shape[0]
    n_chunks = tile // CHUNK

    @pl.when(pl.program_id(1) == 0)
    def _():
        vbuf[0:SUBLANES, :] = jnp.zeros((SUBLANES, w), F32)

    @pl.when(pl.program_id(1) != 0)
    def _():
        vbuf[0:SUBLANES, :] = vbuf[tile:tile + SUBLANES, :]

    ys = []
    for i in range(n_chunks):
        r0 = i * CHUNK
        v0 = SUBLANES + r0
        x = x_ref[r0:r0 + CHUNK, :]
        h = (x * _rms_scale(x) * g_ref[...]).astype(BF16)
        cu = jnp.dot(h, win_ref[:, w:3 * w], preferred_element_type=F32)
        vbuf[v0:v0 + CHUNK, :] = cu[:, 0:w] * cu[:, w:2 * w]
        conv = (cw_ref[2:3, :] * vbuf[v0:v0 + CHUNK, :]
                + cw_ref[1:2, :] * vbuf[v0 - 1:v0 - 1 + CHUNK, :]
                + cw_ref[0:1, :] * vbuf[v0 - 2:v0 - 2 + CHUNK, :])
        z = jnp.dot(h, win_ref[:, 3 * w:4 * w], preferred_element_type=F32)
        gate = conv * (z * jax.nn.sigmoid(z))
        b_gate = jnp.dot(h, win_ref[:, 0:w], preferred_element_type=F32)
        ys.append((b_gate * gate).astype(BF16))
    for i in range(n_chunks):
        r0 = i * CHUNK
        o_ref[r0:r0 + CHUNK, :] = x_ref[r0:r0 + CHUNK, :] + jnp.dot(
            ys[i], wout_ref[...], preferred_element_type=F32)


def _layer_a(x, ln_g, w_in, conv_w, w_out):
    bsz, seq, d = x.shape
    w = w_out.shape[0]
    const = lambda b, j: (0, 0)
    single = pl.Buffered(1)
    return pl.pallas_call(
        functools.partial(_layer_a_kernel, width=w),
        out_shape=jax.ShapeDtypeStruct(x.shape, x.dtype),
        grid=(bsz, seq // SEQ_TILE),
        in_specs=[
            pl.BlockSpec((None, SEQ_TILE, d), lambda b, j: (b, j, 0)),
            pl.BlockSpec((1, d), const),
            pl.BlockSpec((d, 4 * w), const, pipeline_mode=single),
            pl.BlockSpec((CONV_W, w), const),
            pl.BlockSpec((w, d), const, pipeline_mode=single),
        ],
        out_specs=pl.BlockSpec((None, SEQ_TILE, d), lambda b, j: (b, j, 0)),
        scratch_shapes=[pltpu.VMEM((SUBLANES + SEQ_TILE, w), F32)],
        compiler_params=pltpu.CompilerParams(
            dimension_semantics=("arbitrary", "arbitrary"),
            vmem_limit_bytes=VMEM_LIMIT_BYTES),
        name="layer_a_shortconv",
    )(x, ln_g.reshape(1, d), w_in.astype(BF16), conv_w.reshape(CONV_W, w), w_out.astype(BF16))


def _spread(n_slots, n_items):
    return [((k + 1) * n_slots) // (n_items + 1) for k in range(n_items)]


def _layer_b_kernel(x_ref, lnkv_ref, lnb_ref, wkv_ref, kg_ref, cost_ref, sint_ref,
                    winb_ref, qga_ref, qgb_ref, cq_ref, sq_ref, sink_ref, bias_ref, woutb_ref,
                    o_ref, kt_scr, v_scr):
    tile, d_model = x_ref.shape
    kv_w = N_KV_HEADS * HEAD_DIM
    grp_w = GROUP * HEAD_DIM
    first_tile = pl.program_id(1) == 0

    @pl.when(first_tile)
    def _():
        kt_scr[:, :, 0:BLOCK] = jnp.zeros((N_KV_HEADS, grp_w, BLOCK), BF16)
        v_scr[:, 0:BLOCK, :] = jnp.zeros((N_KV_HEADS, BLOCK, grp_w), BF16)

    @pl.when(jnp.logical_not(first_tile))
    def _():
        kt_scr[:, :, 0:BLOCK] = kt_scr[:, :, tile:tile + BLOCK]
        v_scr[:, 0:BLOCK, :] = v_scr[:, tile:tile + BLOCK, :]

    lane_c = lax.broadcasted_iota(jnp.int32, (CHUNK, LANES), 1)
    lo_c = lane_c < HEAD_DIM
    quarter_c = [jnp.logical_and(lane_c >= hh * HALF_DIM, lane_c < (hh + 1) * HALF_DIM) for hh in range(GROUP)]
    lane_g = lax.broadcasted_iota(jnp.int32, (BLOCK, grp_w), 1) % LANES
    quarter_g = [jnp.logical_and(lane_g >= hh * HALF_DIM, lane_g < (hh + 1) * HALF_DIM) for hh in range(GROUP)]
    lo_b = lax.broadcasted_iota(jnp.int32, (BLOCK, LANES), 1) < HEAD_DIM
    ones = jnp.ones((CHUNK, LANES), BF16)

    half_w = d_model // 2
    n_chunks = tile // CHUNK
    n_qb = CHUNK // BLOCK
    n_iter = n_qb * N_KV_HEADS
    h_kv = [None] * n_chunks
    h_b = [None] * n_chunks
    q_groups = [[None] * N_KV_HEADS for _ in range(n_chunks)]
    gates = [[None, None] for _ in range(n_chunks)]
    attn = [[[None] * (2 * N_KV_HEADS) for _ in range(n_qb)] for _ in range(n_chunks)]
    gated = [None] * n_chunks

    def prep(i):
        x = x_ref[i * CHUNK:(i + 1) * CHUNK, :]
        xn = x * _rms_scale(x)
        h_kv[i] = (xn * lnkv_ref[...]).astype(BF16)
        h_b[i] = (xn * lnb_ref[...]).astype(BF16)

    def kv_piece(i):
        k0 = BLOCK + i * CHUNK
        kv = jnp.dot(h_kv[i], wkv_ref[...], preferred_element_type=F32)
        k_t = kv[:, 0:kv_w].T
        cos_t = cost_ref[i]
        sin_t = sint_ref[i]
        for g in range(N_KV_HEADS):
            kh = k_t[g * HEAD_DIM:(g + 1) * HEAD_DIM, :]
            kh = kh * lax.rsqrt(jnp.mean(kh * kh, axis=0, keepdims=True) + EPS) * kg_ref[...]
            r1 = kh[0:ROT_HALF, :]
            r2 = kh[ROT_HALF:ROT_DIM, :]
            set_a = jnp.concatenate([r1 * cos_t - r2 * sin_t, kh[ROT_DIM:ROT_DIM + HALF_DIM - ROT_HALF, :]], axis=0)
            set_b = jnp.concatenate([r2 * cos_t + r1 * sin_t, kh[ROT_DIM + HALF_DIM - ROT_HALF:, :]], axis=0)
            kt_scr[g, :, k0:k0 + CHUNK] = jnp.concatenate([set_a] * GROUP + [set_b] * GROUP, axis=0).astype(BF16)
        for c in range(kv_w // LANES):
            vc = kv[:, kv_w + c * LANES:kv_w + (c + 1) * LANES]
            vr = pltpu.roll(vc, HEAD_DIM, axis=1)
            v_scr[2 * c, k0:k0 + CHUNK, 0:LANES] = jnp.where(lo_c, vc, vr).astype(BF16)
            v_scr[2 * c + 1, k0:k0 + CHUNK, 0:LANES] = jnp.where(lo_c, vr, vc).astype(BF16)
            v_scr[2 * c, k0:k0 + CHUNK, LANES:2 * LANES] = ones
            v_scr[2 * c + 1, k0:k0 + CHUNK, LANES:2 * LANES] = ones

    def q_piece(i, half):
        q = jnp.dot(h_b[i], winb_ref[:, half * half_w:(half + 1) * half_w], preferred_element_type=F32)
        cq = cq_ref[i * CHUNK:(i + 1) * CHUNK, :]
        sq = sq_ref[i * CHUNK:(i + 1) * CHUNK, :]
        for gg in range(half_w // grp_w):
            c0 = q[:, gg * grp_w:gg * grp_w + LANES]
            c1 = q[:, gg * grp_w + LANES:(gg + 1) * grp_w]
            t = c0 * c0 + c1 * c1
            ss = [jnp.sum(jnp.where(quarter_c[hh], t, 0.0), axis=-1, keepdims=True) for hh in range(GROUP)]
            ms = jnp.where(quarter_c[0], ss[0], jnp.where(quarter_c[1], ss[1], jnp.where(quarter_c[2], ss[2], ss[3])))
            rstd = lax.rsqrt(ms * (1.0 / HEAD_DIM) + EPS)
            n0 = c0 * rstd * qga_ref[...]
            n1 = c1 * rstd * qgb_ref[...]
            q_groups[i][half * (half_w // grp_w) + gg] = jnp.concatenate(
                [n0 * cq - n1 * sq, n1 * cq + n0 * sq], axis=1).astype(BF16)

    def z_piece(i, half):
        z = jnp.dot(h_b[i], winb_ref[:, d_model + half * half_w:d_model + (half + 1) * half_w],
                    preferred_element_type=F32)
        gates[i][half] = z * jax.nn.sigmoid(z)

    def o_piece(i, half):
        if gated[i] is None:
            full = jnp.concatenate([jnp.concatenate(attn[i][qb], axis=1) for qb in range(n_qb)], axis=0)
            gated[i] = (full * jnp.concatenate(gates[i], axis=1)).astype(BF16)
        rows = slice(i * CHUNK, (i + 1) * CHUNK)
        cols = slice(half * half_w, (half + 1) * half_w)
        o_ref[rows, cols] = x_ref[rows, cols] + jnp.dot(gated[i], woutb_ref[:, cols], preferred_element_type=F32)

    def scores(i, t):
        qb, g = divmod(t, N_KV_HEADS)
        k_lo = i * CHUNK + qb * BLOCK
        qg = q_groups[i][g][qb * BLOCK:(qb + 1) * BLOCK, :]
        q4 = jnp.concatenate([jnp.where(quarter_g[hh], qg, jnp.zeros_like(qg)) for hh in range(GROUP)], axis=0)
        return jnp.dot(q4, kt_scr[g, :, k_lo:k_lo + 2 * BLOCK], preferred_element_type=F32)

    def attend(i, t, s):
        qb, g = divmod(t, N_KV_HEADS)
        k_lo = i * CHUNK + qb * BLOCK
        if i == 0 and qb == 0:
            bias = bias_ref[jnp.where(first_tile, 0, 1)]
        else:
            bias = bias_ref[1]
        probs = []
        sink_terms = []
        for hh in range(GROUP):
            sink = sink_ref[g * GROUP + hh] * LOG2E
            sm = jnp.minimum(s[hh * BLOCK:(hh + 1) * BLOCK, :], bias)
            m = jnp.max(sm, axis=-1, keepdims=True)
            probs.append(jnp.exp2(sm - m).astype(BF16))
            sink_terms.append(jnp.exp2(sink - m))
        o4 = jnp.dot(jnp.concatenate(probs, axis=0), v_scr[g, k_lo:k_lo + 2 * BLOCK, :],
                     preferred_element_type=F32)
        res = []
        for hh in range(GROUP):
            blk = o4[hh * BLOCK:(hh + 1) * BLOCK, :]
            res.append(blk[:, 0:LANES] * (1.0 / (blk[:, LANES:2 * LANES] + sink_terms[hh])))
        attn[i][qb][2 * g] = jnp.where(lo_b, res[0], res[1])
        attn[i][qb][2 * g + 1] = jnp.where(lo_b, res[2], res[3])

    prep(0)
    kv_piece(0)
    q_piece(0, 0)
    q_piece(0, 1)
    for i in range(n_chunks):
        fillers = []
        if i > 0:
            fillers += [functools.partial(o_piece, i - 1, 0), functools.partial(o_piece, i - 1, 1)]
        fillers += [functools.partial(z_piece, i, 0), functools.partial(z_piece, i, 1)]
        if i + 1 < n_chunks:
            prep(i + 1)
            fillers += [functools.partial(kv_piece, i + 1), functools.partial(q_piece, i + 1, 0),
                        functools.partial(q_piece, i + 1, 1)]
        after = _spread(n_iter, len(fillers))
        pending = [scores(i, t) for t in range(SCORE_LOOKAHEAD)]
        for t in range(n_iter):
            if t + SCORE_LOOKAHEAD < n_iter:
                pending.append(scores(i, t + SCORE_LOOKAHEAD))
            for k, slot in enumerate(after):
                if slot == t:
                    fillers[k]()
            attend(i, t, pending.pop(0))
    o_piece(n_chunks - 1, 0)
    o_piece(n_chunks - 1, 1)


def _rope_tables(seq):
    pos = np.arange(seq, dtype=np.float32)
    inv = np.float32(ROPE_THETA) ** (-np.arange(0, ROT_DIM, 2, dtype=np.float32) / np.float32(ROT_DIM))
    ang = pos[:, None] * inv[None, :]
    return np.cos(ang), np.sin(ang)


def _query_column_order():
    order = []
    for g in range(N_KV_HEADS):
        for dims in (DIMS_A, DIMS_B):
            for hh in range(GROUP):
                order.append((g * GROUP + hh) * HEAD_DIM + dims)
    return np.concatenate(order)


def _layer_b(x, ln_kv, w_kv, k_norm, ln_b, w_in_b, q_norm, sinks, w_out_b):
    bsz, seq, d = x.shape
    kv2 = w_kv.shape[1]
    grp_w = GROUP * HEAD_DIM
    scale = LOG2E / math.sqrt(HEAD_DIM)

    cos, sin = _rope_tables(seq)
    cos_t = np.ascontiguousarray(cos.T.reshape(ROT_HALF, seq // CHUNK, CHUNK).transpose(1, 0, 2))
    sin_t = np.ascontiguousarray(sin.T.reshape(ROT_HALF, seq // CHUNK, CHUNK).transpose(1, 0, 2))
    jj = np.arange(LANES) % HALF_DIM
    rot = jj < ROT_HALF
    cq = (np.where(rot, cos[:, jj % ROT_HALF], 1.0) * scale).astype(np.float32)
    sq = (np.where(rot, sin[:, jj % ROT_HALF], 0.0) * scale).astype(np.float32)
    qg_a = jnp.tile(q_norm[DIMS_A], GROUP).reshape(1, LANES)
    qg_b = jnp.tile(q_norm[DIMS_B], GROUP).reshape(1, LANES)
    w_qz = jnp.concatenate([w_in_b[:, :d][:, _query_column_order()], w_in_b[:, d:]], axis=1).astype(BF16)

    a_idx = np.arange(BLOCK)[:, None]
    c_idx = np.arange(2 * BLOCK)[None, :]
    rel_ok = (c_idx <= a_idx + BLOCK) & (c_idx > a_idx + BLOCK - WINDOW)
    big = np.finfo(np.float32).max
    bias_rest = np.where(rel_ok, big, NEG_INF).astype(np.float32)
    bias_first = np.where(rel_ok & (c_idx >= BLOCK), big, NEG_INF).astype(np.float32)
    bias = np.stack([bias_first, bias_rest])

    const2 = lambda b, j: (0, 0)
    const3 = lambda b, j: (0, 0, 0)
    single = pl.Buffered(1)
    return pl.pallas_call(
        _layer_b_kernel,
        out_shape=jax.ShapeDtypeStruct(x.shape, x.dtype),
        grid=(bsz, seq // SEQ_TILE),
        in_specs=[
            pl.BlockSpec((None, SEQ_TILE, d), lambda b, j: (b, j, 0)),
            pl.BlockSpec((1, d), const2),
            pl.BlockSpec((1, d), const2),
            pl.BlockSpec((d, kv2), const2, pipeline_mode=single),
            pl.BlockSpec((HEAD_DIM, 1), const2),
            pl.BlockSpec((SEQ_TILE // CHUNK, ROT_HALF, CHUNK), lambda b, j: (j, 0, 0)),
            pl.BlockSpec((SEQ_TILE // CHUNK, ROT_HALF, CHUNK), lambda b, j: (j, 0, 0)),
            pl.BlockSpec((d, 2 * d), const2, pipeline_mode=single),
            pl.BlockSpec((1, LANES), const2),
            pl.BlockSpec((1, LANES), const2),
            pl.BlockSpec((SEQ_TILE, LANES), lambda b, j: (j, 0)),
            pl.BlockSpec((SEQ_TILE, LANES), lambda b, j: (j, 0)),
            pl.BlockSpec(memory_space=pltpu.SMEM),
            pl.BlockSpec((2, BLOCK, 2 * BLOCK), const3),
            pl.BlockSpec((d, d), const2, pipeline_mode=single),
        ],
        out_specs=pl.BlockSpec((None, SEQ_TILE, d), lambda b, j: (b, j, 0)),
        scratch_shapes=[
            pltpu.VMEM((N_KV_HEADS, grp_w, BLOCK + SEQ_TILE), BF16),
            pltpu.VMEM((N_KV_HEADS, BLOCK + SEQ_TILE, grp_w), BF16),
        ],
        compiler_params=pltpu.CompilerParams(
            dimension_semantics=("arbitrary", "arbitrary"),
            vmem_limit_bytes=VMEM_LIMIT_BYTES),
        name="layer_b_swa",
    )(x, ln_kv.reshape(1, d), ln_b.reshape(1, d), w_kv.astype(BF16), k_norm.reshape(HEAD_DIM, 1),
      cos_t, sin_t, w_qz, qg_a, qg_b, cq, sq, sinks.astype(F32), bias, w_out_b.astype(BF16))


def kernel(x, ln_a, w_in_a, conv_a, w_out_a, ln_kv, w_kv, k_norm, ln_b, w_in_b, q_norm, sinks, w_out_b):
    n_a = ln_a.shape[0]
    n_b = ln_b.shape[0]
    assert n_b == 1, "the shared K/V is recomputed inside the single attention layer"
    for i in range(n_a):
        x = _layer_a(x, ln_a[i], w_in_a[i], conv_a[i], w_out_a[i])
    for i in range(n_b):
        x = _layer_b(x, ln_kv, w_kv, k_norm, ln_b[i], w_in_b[i], q_norm[i], sinks[i], w_out_b[i])
    return x
```

```python
import functools
import math

import numpy as np

import jax
import jax.numpy as jnp
from jax import lax
from jax.experimental import pallas as pl
from jax.experimental.pallas import tpu as pltpu

F32 = jnp.float32
BF16 = jnp.bfloat16

HEAD_DIM = 64
N_KV_HEADS = 4
GROUP = 4
N_HEADS = N_KV_HEADS * GROUP
WINDOW = 128
BLOCK = 128
ROT_DIM = HEAD_DIM // 4
ROT_HALF = ROT_DIM // 2
ROPE_THETA = 500000.0
EPS = 1e-6
NEG_INF = -1e30
LOG2E = math.log2(math.e)
CONV_W = 3

LANES = 128
SUBLANES = 8
VMEM_LIMIT_BYTES = 56 * 1024 * 1024

SEQ_TILE = 1024
CHUNK = 512
SCORE_LOOKAHEAD = 2
DENSE_PIECE = 512

HALF_DIM = HEAD_DIM // 2
DIMS_A = np.concatenate([np.arange(0, ROT_HALF), np.arange(ROT_DIM, ROT_DIM + HALF_DIM - ROT_HALF)])
DIMS_B = np.concatenate([np.arange(ROT_HALF, ROT_DIM), np.arange(ROT_DIM + HALF_DIM - ROT_HALF, HEAD_DIM)])


def _rms_scale(x):
    return lax.rsqrt(jnp.mean(x * x, axis=-1, keepdims=True) + EPS)


def _layer_a_kernel(x_ref, g_ref, win_ref, cw_ref, wout_ref, o_ref, vbuf, *, width):
    w = width
    tile = x_ref.shape[0]
    n_chunks = tile // CHUNK

    @pl.when(pl.program_id(1) == 0)
    def _():
        vbuf[0:SUBLANES, :] = jnp.zeros((SUBLANES, w), F32)

    @pl.when(pl.program_id(1) != 0)
    def _():
        vbuf[0:SUBLANES, :] = vbuf[tile:tile + SUBLANES, :]

    ys = []
    for i in range(n_chunks):
        r0 = i * CHUNK
        v0 = SUBLANES + r0
        x = x_ref[r0:r0 + CHUNK, :]
        h = (x * _rms_scale(x) * g_ref[...]).astype(BF16)
        cu = jnp.dot(h, win_ref[:, w:3 * w], preferred_element_type=F32)
        vbuf[v0:v0 + CHUNK, :] = cu[:, 0:w] * cu[:, w:2 * w]
        conv = (cw_ref[2:3, :] * vbuf[v0:v0 + CHUNK, :]
                + cw_ref[1:2, :] * vbuf[v0 - 1:v0 - 1 + CHUNK, :]
                + cw_ref[0:1, :] * vbuf[v0 - 2:v0 - 2 + CHUNK, :])
        z = jnp.dot(h, win_ref[:, 3 * w:4 * w], preferred_element_type=F32)
        gate = conv * (z * jax.nn.sigmoid(z))
        b_gate = jnp.dot(h, win_ref[:, 0:w], preferred_element_type=F32)
        ys.append((b_gate * gate).astype(BF16))
    for i in range(n_chunks):
        r0 = i * CHUNK
        o_ref[r0:r0 + CHUNK, :] = x_ref[r0:r0 + CHUNK, :] + jnp.dot(
            ys[i], wout_ref[...], preferred_element_type=F32)


def _layer_a(x, ln_g, w_in, conv_w, w_out):
    bsz, seq, d = x.shape
    w = w_out.shape[0]
    const = lambda b, j: (0, 0)
    single = pl.Buffered(1)
    return pl.pallas_call(
        functools.partial(_layer_a_kernel, width=w),
        out_shape=jax.ShapeDtypeStruct(x.shape, x.dtype),
        grid=(bsz, seq // SEQ_TILE),
        in_specs=[
            pl.BlockSpec((None, SEQ_TILE, d), lambda b, j: (b, j, 0)),
            pl.BlockSpec((1, d), const),
            pl.BlockSpec((d, 4 * w), const, pipeline_mode=single),
            pl.BlockSpec((CONV_W, w), const),
            pl.BlockSpec((w, d), const, pipeline_mode=single),
        ],
        out_specs=pl.BlockSpec((None, SEQ_TILE, d), lambda b, j: (b, j, 0)),
        scratch_shapes=[pltpu.VMEM((SUBLANES + SEQ_TILE, w), F32)],
        compiler_params=pltpu.CompilerParams(
            dimension_semantics=("arbitrary", "arbitrary"),
            vmem_limit_bytes=VMEM_LIMIT_BYTES),
        name="layer_a_shortconv",
    )(x, ln_g.reshape(1, d), w_in.astype(BF16), conv_w.reshape(CONV_W, w), w_out.astype(BF16))


def _spread(n_slots, n_items):
    return [((k + 1) * n_slots) // (n_items + 1) for k in range(n_items)]


def _layer_b_kernel(x_ref, lnkv_ref, lnb_ref, wkv_ref, kg_ref, cost_ref, sint_ref,
                    winb_ref, qga_ref, qgb_ref, cq_ref, sq_ref, sink_ref, bias_ref, woutb_ref,
                    o_ref, kt_scr, v_scr):
    tile, d_model = x_ref.shape
    kv_w = N_KV_HEADS * HEAD_DIM
    grp_w = GROUP * HEAD_DIM
    first_tile = pl.program_id(1) == 0

    @pl.when(first_tile)
    def _():
        kt_scr[:, :, 0:BLOCK] = jnp.zeros((N_KV_HEADS, grp_w, BLOCK), BF16)
        v_scr[:, 0:BLOCK, :] = jnp.zeros((N_KV_HEADS, BLOCK, grp_w), BF16)

    @pl.when(jnp.logical_not(first_tile))
    def _():
        kt_scr[:, :, 0:BLOCK] = kt_scr[:, :, tile:tile + BLOCK]
        v_scr[:, 0:BLOCK, :] = v_scr[:, tile:tile + BLOCK, :]

    lane_c = lax.broadcasted_iota(jnp.int32, (CHUNK, LANES), 1)
    lo_c = lane_c < HEAD_DIM
    quarter_c = [jnp.logical_and(lane_c >= hh * HALF_DIM, lane_c < (hh + 1) * HALF_DIM) for hh in range(GROUP)]
    lane_g = lax.broadcasted_iota(jnp.int32, (BLOCK, grp_w), 1) % LANES
    quarter_g = [jnp.logical_and(lane_g >= hh * HALF_DIM, lane_g < (hh + 1) * HALF_DIM) for hh in range(GROUP)]
    lo_b = lax.broadcasted_iota(jnp.int32, (BLOCK, LANES), 1) < HEAD_DIM
    lo_row = lax.broadcasted_iota(jnp.int32, (1, LANES), 1) < HEAD_DIM
    ones = jnp.ones((CHUNK, LANES), BF16)

    piece_w = DENSE_PIECE
    n_pieces = d_model // piece_w
    n_chunks = tile // CHUNK
    n_qb = CHUNK // BLOCK
    n_iter = n_qb * N_KV_HEADS
    h_kv = [None] * n_chunks
    h_b = [None] * n_chunks
    q_groups = [[None] * N_KV_HEADS for _ in range(n_chunks)]
    gates = [[None] * n_pieces for _ in range(n_chunks)]
    attn = [[[None] * (2 * N_KV_HEADS) for _ in range(n_qb)] for _ in range(n_chunks)]
    gated = [None] * n_chunks

    def prep(i):
        x = x_ref[i * CHUNK:(i + 1) * CHUNK, :]
        xn = x * _rms_scale(x)
        h_kv[i] = (xn * lnkv_ref[...]).astype(BF16)
        h_b[i] = (xn * lnb_ref[...]).astype(BF16)

    def k_piece(i):
        k0 = BLOCK + i * CHUNK
        k_t = jnp.dot(h_kv[i], wkv_ref[:, 0:kv_w], preferred_element_type=F32).T
        cos_t = cost_ref[i]
        sin_t = sint_ref[i]
        for g in range(N_KV_HEADS):
            kh = k_t[g * HEAD_DIM:(g + 1) * HEAD_DIM, :]
            kh = kh * lax.rsqrt(jnp.mean(kh * kh, axis=0, keepdims=True) + EPS) * kg_ref[...]
            r1 = kh[0:ROT_HALF, :]
            r2 = kh[ROT_HALF:ROT_DIM, :]
            set_a = jnp.concatenate([r1 * cos_t - r2 * sin_t, kh[ROT_DIM:ROT_DIM + HALF_DIM - ROT_HALF, :]], axis=0)
            set_b = jnp.concatenate([r2 * cos_t + r1 * sin_t, kh[ROT_DIM + HALF_DIM - ROT_HALF:, :]], axis=0)
            kt_scr[g, :, k0:k0 + CHUNK] = jnp.concatenate([set_a] * GROUP + [set_b] * GROUP, axis=0).astype(BF16)

    def v_piece(i):
        k0 = BLOCK + i * CHUNK
        v = jnp.dot(h_kv[i], wkv_ref[:, kv_w:2 * kv_w], preferred_element_type=F32)
        for c in range(kv_w // LANES):
            vc = v[:, c * LANES:(c + 1) * LANES]
            vr = pltpu.roll(vc, HEAD_DIM, axis=1)
            v_scr[2 * c, k0:k0 + CHUNK, 0:LANES] = jnp.where(lo_c, vc, vr).astype(BF16)
            v_scr[2 * c + 1, k0:k0 + CHUNK, 0:LANES] = jnp.where(lo_c, vr, vc).astype(BF16)
            v_scr[2 * c, k0:k0 + CHUNK, LANES:2 * LANES] = ones
            v_scr[2 * c + 1, k0:k0 + CHUNK, LANES:2 * LANES] = ones

    def q_piece(i, p):
        q = jnp.dot(h_b[i], winb_ref[:, p * piece_w:(p + 1) * piece_w], preferred_element_type=F32)
        cq = cq_ref[i * CHUNK:(i + 1) * CHUNK, :]
        sq = sq_ref[i * CHUNK:(i + 1) * CHUNK, :]
        for gg in range(piece_w // grp_w):
            c0 = q[:, gg * grp_w:gg * grp_w + LANES]
            c1 = q[:, gg * grp_w + LANES:(gg + 1) * grp_w]
            t = c0 * c0 + c1 * c1
            ss = [jnp.sum(jnp.where(quarter_c[hh], t, 0.0), axis=-1, keepdims=True) for hh in range(GROUP)]
            ms = jnp.where(quarter_c[0], ss[0], jnp.where(quarter_c[1], ss[1], jnp.where(quarter_c[2], ss[2], ss[3])))
            rstd = lax.rsqrt(ms * (1.0 / HEAD_DIM) + EPS)
            n0 = c0 * rstd * qga_ref[...]
            n1 = c1 * rstd * qgb_ref[...]
            q_groups[i][p * (piece_w // grp_w) + gg] = jnp.concatenate(
                [n0 * cq - n1 * sq, n1 * cq + n0 * sq], axis=1).astype(BF16)

    def z_piece(i, p):
        z = jnp.dot(h_b[i], winb_ref[:, d_model + p * piece_w:d_model + (p + 1) * piece_w],
                    preferred_element_type=F32)
        gates[i][p] = z * jax.nn.sigmoid(z)

    def o_piece(i, p):
        if gated[i] is None:
            full = jnp.concatenate([jnp.concatenate(attn[i][qb], axis=1) for qb in range(n_qb)], axis=0)
            gated[i] = (full * jnp.concatenate(gates[i], axis=1)).astype(BF16)
        rows = slice(i * CHUNK, (i + 1) * CHUNK)
        cols = slice(p * piece_w, (p + 1) * piece_w)
        o_ref[rows, cols] = x_ref[rows, cols] + jnp.dot(gated[i], woutb_ref[:, cols], preferred_element_type=F32)

    def scores(i, t):
        qb, g = divmod(t, N_KV_HEADS)
        k_lo = i * CHUNK + qb * BLOCK
        qg = q_groups[i][g][qb * BLOCK:(qb + 1) * BLOCK, :]
        q4 = jnp.concatenate([jnp.where(quarter_g[hh], qg, jnp.zeros_like(qg)) for hh in range(GROUP)], axis=0)
        return jnp.dot(q4, kt_scr[g, :, k_lo:k_lo + 2 * BLOCK], preferred_element_type=F32)

    def attend(i, t, s):
        qb, g = divmod(t, N_KV_HEADS)
        k_lo = i * CHUNK + qb * BLOCK
        if i == 0 and qb == 0:
            bias = bias_ref[jnp.where(first_tile, 0, 1)]
        else:
            bias = bias_ref[1]
        probs = []
        row_max = []
        for hh in range(GROUP):
            sm = jnp.minimum(s[hh * BLOCK:(hh + 1) * BLOCK, :], bias)
            m = jnp.max(sm, axis=-1, keepdims=True)
            probs.append(jnp.exp2(sm - m).astype(BF16))
            row_max.append(m)
        o4 = jnp.dot(jnp.concatenate(probs, axis=0), v_scr[g, k_lo:k_lo + 2 * BLOCK, :],
                     preferred_element_type=F32)
        for pair in range(GROUP // 2):
            h0, h1 = 2 * pair, 2 * pair + 1
            sink = jnp.where(lo_row, sink_ref[g * GROUP + h0] * LOG2E, sink_ref[g * GROUP + h1] * LOG2E)
            blk0 = o4[h0 * BLOCK:(h0 + 1) * BLOCK, :]
            blk1 = o4[h1 * BLOCK:(h1 + 1) * BLOCK, :]
            den = (jnp.where(lo_b, blk0[:, LANES:2 * LANES], blk1[:, LANES:2 * LANES])
                   + jnp.exp2(sink - jnp.where(lo_b, row_max[h0], row_max[h1])))
            attn[i][qb][2 * g + pair] = jnp.where(lo_b, blk0[:, 0:LANES], blk1[:, 0:LANES]) * (1.0 / den)

    prep(0)
    k_piece(0)
    v_piece(0)
    for p in range(n_pieces):
        q_piece(0, p)
    for i in range(n_chunks):
        fillers = []
        if i > 0:
            fillers += [functools.partial(o_piece, i - 1, p) for p in range(n_pieces)]
        fillers += [functools.partial(z_piece, i, p) for p in range(n_pieces)]
        if i + 1 < n_chunks:
            prep(i + 1)
            fillers += [functools.partial(k_piece, i + 1), functools.partial(v_piece, i + 1)]
            fillers += [functools.partial(q_piece, i + 1, p) for p in range(n_pieces)]
        after = _spread(n_iter, len(fillers))
        pending = [scores(i, t) for t in range(SCORE_LOOKAHEAD)]
        for t in range(n_iter):
            if t + SCORE_LOOKAHEAD < n_iter:
                pending.append(scores(i, t + SCORE_LOOKAHEAD))
            for k, slot in enumerate(after):
                if slot == t:
                    fillers[k]()
            attend(i, t, pending.pop(0))
    for p in range(n_pieces):
        o_piece(n_chunks - 1, p)


def _rope_tables(seq):
    pos = np.arange(seq, dtype=np.float32)
    inv = np.float32(ROPE_THETA) ** (-np.arange(0, ROT_DIM, 2, dtype=np.float32) / np.float32(ROT_DIM))
    ang = pos[:, None] * inv[None, :]
    return np.cos(ang), np.sin(ang)


def _dim_sets(t):
    mid = ROT_DIM + HALF_DIM - ROT_HALF
    set_a = jnp.concatenate([t[..., 0:ROT_HALF], t[..., ROT_DIM:mid]], axis=-1)
    set_b = jnp.concatenate([t[..., ROT_HALF:ROT_DIM], t[..., mid:HEAD_DIM]], axis=-1)
    return set_a, set_b


def _permute_query_columns(w_q):
    d_in = w_q.shape[0]
    set_a, set_b = _dim_sets(w_q.reshape(d_in, N_KV_HEADS, GROUP, HEAD_DIM))
    return jnp.stack([set_a, set_b], axis=2).reshape(d_in, N_HEADS * HEAD_DIM)


def _layer_b(x, ln_kv, w_kv, k_norm, ln_b, w_in_b, q_norm, sinks, w_out_b):
    bsz, seq, d = x.shape
    kv2 = w_kv.shape[1]
    grp_w = GROUP * HEAD_DIM
    scale = LOG2E / math.sqrt(HEAD_DIM)

    cos, sin = _rope_tables(seq)
    cos_t = np.ascontiguousarray(cos.T.reshape(ROT_HALF, seq // CHUNK, CHUNK).transpose(1, 0, 2))
    sin_t = np.ascontiguousarray(sin.T.reshape(ROT_HALF, seq // CHUNK, CHUNK).transpose(1, 0, 2))
    jj = np.arange(LANES) % HALF_DIM
    rot = jj < ROT_HALF
    cq = (np.where(rot, cos[:, jj % ROT_HALF], 1.0) * scale).astype(np.float32)
    sq = (np.where(rot, sin[:, jj % ROT_HALF], 0.0) * scale).astype(np.float32)
    gain_a, gain_b = _dim_sets(q_norm)
    qg_a = jnp.tile(gain_a, GROUP).reshape(1, LANES)
    qg_b = jnp.tile(gain_b, GROUP).reshape(1, LANES)
    w_qz = jnp.concatenate([_permute_query_columns(w_in_b[:, :d]), w_in_b[:, d:]], axis=1).astype(BF16)

    a_idx = np.arange(BLOCK)[:, None]
    c_idx = np.arange(2 * BLOCK)[None, :]
    rel_ok = (c_idx <= a_idx + BLOCK) & (c_idx > a_idx + BLOCK - WINDOW)
    big = np.finfo(np.float32).max
    bias_rest = np.where(rel_ok, big, NEG_INF).astype(np.float32)
    bias_first = np.where(rel_ok & (c_idx >= BLOCK), big, NEG_INF).astype(np.float32)
    bias = np.stack([bias_first, bias_rest])

    const2 = lambda b, j: (0, 0)
    const3 = lambda b, j: (0, 0, 0)
    single = pl.Buffered(1)
    return pl.pallas_call(
        _layer_b_kernel,
        out_shape=jax.ShapeDtypeStruct(x.shape, x.dtype),
        grid=(bsz, seq // SEQ_TILE),
        in_specs=[
            pl.BlockSpec((None, SEQ_TILE, d), lambda b, j: (b, j, 0)),
            pl.BlockSpec((1, d), const2),
            pl.BlockSpec((1, d), const2),
            pl.BlockSpec((d, kv2), const2, pipeline_mode=single),
            pl.BlockSpec((HEAD_DIM, 1), const2),
            pl.BlockSpec((SEQ_TILE // CHUNK, ROT_HALF, CHUNK), lambda b, j: (j, 0, 0)),
            pl.BlockSpec((SEQ_TILE // CHUNK, ROT_HALF, CHUNK), lambda b, j: (j, 0, 0)),
            pl.BlockSpec((d, 2 * d), const2, pipeline_mode=single),
            pl.BlockSpec((1, LANES), const2),
            pl.BlockSpec((1, LANES), const2),
            pl.BlockSpec((SEQ_TILE, LANES), lambda b, j: (j, 0)),
            pl.BlockSpec((SEQ_TILE, LANES), lambda b, j: (j, 0)),
            pl.BlockSpec(memory_space=pltpu.SMEM),
            pl.BlockSpec((2, BLOCK, 2 * BLOCK), const3),
            pl.BlockSpec((d, d), const2, pipeline_mode=single),
        ],
        out_specs=pl.BlockSpec((None, SEQ_TILE, d), lambda b, j: (b, j, 0)),
        scratch_shapes=[
            pltpu.VMEM((N_KV_HEADS, grp_w, BLOCK + SEQ_TILE), BF16),
            pltpu.VMEM((N_KV_HEADS, BLOCK + SEQ_TILE, grp_w), BF16),
        ],
        compiler_params=pltpu.CompilerParams(
            dimension_semantics=("arbitrary", "arbitrary"),
            vmem_limit_bytes=VMEM_LIMIT_BYTES),
        name="layer_b_swa",
    )(x, ln_kv.reshape(1, d), ln_b.reshape(1, d), w_kv.astype(BF16), k_norm.reshape(HEAD_DIM, 1),
      cos_t, sin_t, w_qz, qg_a, qg_b, cq, sq, sinks.astype(F32), bias, w_out_b.astype(BF16))


def kernel(x, ln_a, w_in_a, conv_a, w_out_a, ln_kv, w_kv, k_norm, ln_b, w_in_b, q_norm, sinks, w_out_b):
    n_a = ln_a.shape[0]
    n_b = ln_b.shape[0]
    assert n_b == 1, "the shared K/V is recomputed inside the single attention layer"
    for i in range(n_a):
        x = _layer_a(x, ln_a[i], w_in_a[i], conv_a[i], w_out_a[i])
    for i in range(n_b):
        x = _layer_b(x, ln_kv, w_kv, k_norm, ln_b[i], w_in_b[i], q_norm[i], sinks[i], w_out_b[i])
    return x
```

```python
import functools
import math

import numpy as np

import jax
import jax.numpy as jnp
from jax import lax
from jax.experimental import pallas as pl
from jax.experimental.pallas import tpu as pltpu

F32 = jnp.float32
BF16 = jnp.bfloat16

HEAD_DIM = 64
N_KV_HEADS = 4
GROUP = 4
N_HEADS = N_KV_HEADS * GROUP
WINDOW = 128
BLOCK = 128
ROT_DIM = HEAD_DIM // 4
ROT_HALF = ROT_DIM // 2
ROPE_THETA = 500000.0
EPS = 1e-6
NEG_INF = -1e30
LOG2E = math.log2(math.e)
CONV_W = 3

LANES = 128
SUBLANES = 8
VMEM_LIMIT_BYTES = 56 * 1024 * 1024

SEQ_TILE = 1024
CHUNK = 512
SCORE_LOOKAHEAD = 2
DENSE_PIECE = 512

HALF_DIM = HEAD_DIM // 2
DIMS_A = np.concatenate([np.arange(0, ROT_HALF), np.arange(ROT_DIM, ROT_DIM + HALF_DIM - ROT_HALF)])
DIMS_B = np.concatenate([np.arange(ROT_HALF, ROT_DIM), np.arange(ROT_DIM + HALF_DIM - ROT_HALF, HEAD_DIM)])


def _rms_scale(x):
    return lax.rsqrt(jnp.mean(x * x, axis=-1, keepdims=True) + EPS)


def _layer_a_kernel(x_ref, g_ref, win_ref, cw_ref, wout_ref, o_ref, vbuf, *, width):
    w = width
    tile = x_ref.shape[0]
    n_chunks = tile // CHUNK

    @pl.when(pl.program_id(1) == 0)
    def _():
        vbuf[0:SUBLANES, :] = jnp.zeros((SUBLANES, w), F32)

    @pl.when(pl.program_id(1) != 0)
    def _():
        vbuf[0:SUBLANES, :] = vbuf[tile:tile + SUBLANES, :]

    ys = []
    for i in range(n_chunks):
        r0 = i * CHUNK
        v0 = SUBLANES + r0
        x = x_ref[r0:r0 + CHUNK, :]
        h = (x * _rms_scale(x) * g_ref[...]).astype(BF16)
        cu = jnp.dot(h, win_ref[:, w:3 * w], preferred_element_type=F32)
        vbuf[v0:v0 + CHUNK, :] = cu[:, 0:w] * cu[:, w:2 * w]
        conv = (cw_ref[2] * vbuf[v0:v0 + CHUNK, :]
                + cw_ref[1] * vbuf[v0 - 1:v0 - 1 + CHUNK, :]
                + cw_ref[0] * vbuf[v0 - 2:v0 - 2 + CHUNK, :])
        z = jnp.dot(h, win_ref[:, 3 * w:4 * w], preferred_element_type=F32)
        gate = conv * (z * jax.nn.sigmoid(z))
        b_gate = jnp.dot(h, win_ref[:, 0:w], preferred_element_type=F32)
        ys.append((b_gate * gate).astype(BF16))
    for i in range(n_chunks):
        r0 = i * CHUNK
        o_ref[r0:r0 + CHUNK, :] = x_ref[r0:r0 + CHUNK, :] + jnp.dot(
            ys[i], wout_ref[...], preferred_element_type=F32)


def _layer_a(x, ln_g, w_in, conv_w, w_out):
    bsz, seq, d = x.shape
    w = w_out.shape[0]
    const = lambda b, j: (0, 0)
    single = pl.Buffered(1)
    return pl.pallas_call(
        functools.partial(_layer_a_kernel, width=w),
        out_shape=jax.ShapeDtypeStruct(x.shape, x.dtype),
        grid=(bsz, seq // SEQ_TILE),
        in_specs=[
            pl.BlockSpec((None, SEQ_TILE, d), lambda b, j: (b, j, 0)),
            pl.BlockSpec((1, d), const),
            pl.BlockSpec((d, 4 * w), const, pipeline_mode=single),
            pl.BlockSpec((CONV_W, 1, w), lambda b, j: (0, 0, 0)),
            pl.BlockSpec((w, d), const, pipeline_mode=single),
        ],
        out_specs=pl.BlockSpec((None, SEQ_TILE, d), lambda b, j: (b, j, 0)),
        scratch_shapes=[pltpu.VMEM((SUBLANES + SEQ_TILE, w), F32)],
        compiler_params=pltpu.CompilerParams(
            dimension_semantics=("arbitrary", "arbitrary"),
            vmem_limit_bytes=VMEM_LIMIT_BYTES),
        name="layer_a_shortconv",
    )(x, ln_g.reshape(1, d), w_in.astype(BF16), conv_w, w_out.astype(BF16))


def _spread(n_slots, n_items):
    return [((k + 1) * n_slots) // (n_items + 1) for k in range(n_items)]


def _layer_b_kernel(x_ref, lnkv_ref, lnb_ref, wkv_ref, kg_ref, cost_ref, sint_ref,
                    winb_ref, perm_ref, qga_ref, qgb_ref, cq_ref, sq_ref, sink_ref, bias_ref, woutb_ref,
                    o_ref, kt_scr, v_scr, wq_scr):
    tile, d_model = x_ref.shape
    kv_w = N_KV_HEADS * HEAD_DIM
    grp_w = GROUP * HEAD_DIM
    first_tile = pl.program_id(1) == 0

    @pl.when(jnp.logical_and(pl.program_id(0) == 0, first_tile))
    def _():
        for g in range(N_KV_HEADS):
            cols = slice(g * grp_w, (g + 1) * grp_w)
            wq_scr[:, cols] = jnp.dot(winb_ref[:, cols], perm_ref[...], preferred_element_type=F32).astype(BF16)

    @pl.when(first_tile)
    def _():
        kt_scr[:, :, 0:BLOCK] = jnp.zeros((N_KV_HEADS, grp_w, BLOCK), BF16)
        v_scr[:, 0:BLOCK, :] = jnp.zeros((N_KV_HEADS, BLOCK, grp_w), BF16)

    @pl.when(jnp.logical_not(first_tile))
    def _():
        kt_scr[:, :, 0:BLOCK] = kt_scr[:, :, tile:tile + BLOCK]
        v_scr[:, 0:BLOCK, :] = v_scr[:, tile:tile + BLOCK, :]

    lane_c = lax.broadcasted_iota(jnp.int32, (CHUNK, LANES), 1)
    lo_c = lane_c < HEAD_DIM
    quarter_c = [jnp.logical_and(lane_c >= hh * HALF_DIM, lane_c < (hh + 1) * HALF_DIM) for hh in range(GROUP)]
    lane_g = lax.broadcasted_iota(jnp.int32, (BLOCK, grp_w), 1) % LANES
    quarter_g = [jnp.logical_and(lane_g >= hh * HALF_DIM, lane_g < (hh + 1) * HALF_DIM) for hh in range(GROUP)]
    lo_b = lax.broadcasted_iota(jnp.int32, (BLOCK, LANES), 1) < HEAD_DIM
    lo_row = lax.broadcasted_iota(jnp.int32, (1, LANES), 1) < HEAD_DIM
    ones = jnp.ones((CHUNK, LANES), BF16)

    piece_w = DENSE_PIECE
    n_pieces = d_model // piece_w
    n_chunks = tile // CHUNK
    n_qb = CHUNK // BLOCK
    n_iter = n_qb * N_KV_HEADS
    h_kv = [None] * n_chunks
    h_b = [None] * n_chunks
    q_groups = [[None] * N_KV_HEADS for _ in range(n_chunks)]
    gates = [[None] * n_pieces for _ in range(n_chunks)]
    attn = [[[None] * (2 * N_KV_HEADS) for _ in range(n_qb)] for _ in range(n_chunks)]
    gated = [None] * n_chunks

    def prep(i):
        x = x_ref[i * CHUNK:(i + 1) * CHUNK, :]
        xn = x * _rms_scale(x)
        h_kv[i] = (xn * lnkv_ref[...]).astype(BF16)
        h_b[i] = (xn * lnb_ref[...]).astype(BF16)

    def k_piece(i):
        k0 = BLOCK + i * CHUNK
        k_t = jnp.dot(h_kv[i], wkv_ref[:, 0:kv_w], preferred_element_type=F32).T
        cos_t = cost_ref[i]
        sin_t = sint_ref[i]
        for g in range(N_KV_HEADS):
            kh = k_t[g * HEAD_DIM:(g + 1) * HEAD_DIM, :]
            kh = kh * lax.rsqrt(jnp.mean(kh * kh, axis=0, keepdims=True) + EPS) * kg_ref[...]
            r1 = kh[0:ROT_HALF, :]
            r2 = kh[ROT_HALF:ROT_DIM, :]
            set_a = jnp.concatenate([r1 * cos_t - r2 * sin_t, kh[ROT_DIM:ROT_DIM + HALF_DIM - ROT_HALF, :]], axis=0)
            set_b = jnp.concatenate([r2 * cos_t + r1 * sin_t, kh[ROT_DIM + HALF_DIM - ROT_HALF:, :]], axis=0)
            kt_scr[g, :, k0:k0 + CHUNK] = jnp.concatenate([set_a] * GROUP + [set_b] * GROUP, axis=0).astype(BF16)

    def v_piece(i):
        k0 = BLOCK + i * CHUNK
        v = jnp.dot(h_kv[i], wkv_ref[:, kv_w:2 * kv_w], preferred_element_type=F32)
        for c in range(kv_w // LANES):
            vc = v[:, c * LANES:(c + 1) * LANES]
            vr = pltpu.roll(vc, HEAD_DIM, axis=1)
            v_scr[2 * c, k0:k0 + CHUNK, 0:LANES] = jnp.where(lo_c, vc, vr).astype(BF16)
            v_scr[2 * c + 1, k0:k0 + CHUNK, 0:LANES] = jnp.where(lo_c, vr, vc).astype(BF16)
            v_scr[2 * c, k0:k0 + CHUNK, LANES:2 * LANES] = ones
            v_scr[2 * c + 1, k0:k0 + CHUNK, LANES:2 * LANES] = ones

    def q_piece(i, p):
        q = jnp.dot(h_b[i], wq_scr[:, p * piece_w:(p + 1) * piece_w], preferred_element_type=F32)
        cq = cq_ref[i * CHUNK:(i + 1) * CHUNK, :]
        sq = sq_ref[i * CHUNK:(i + 1) * CHUNK, :]
        for gg in range(piece_w // grp_w):
            c0 = q[:, gg * grp_w:gg * grp_w + LANES]
            c1 = q[:, gg * grp_w + LANES:(gg + 1) * grp_w]
            t = c0 * c0 + c1 * c1
            ss = [jnp.sum(jnp.where(quarter_c[hh], t, 0.0), axis=-1, keepdims=True) for hh in range(GROUP)]
            ms = jnp.where(quarter_c[0], ss[0], jnp.where(quarter_c[1], ss[1], jnp.where(quarter_c[2], ss[2], ss[3])))
            rstd = lax.rsqrt(ms * (1.0 / HEAD_DIM) + EPS)
            n0 = c0 * rstd * qga_ref[...]
            n1 = c1 * rstd * qgb_ref[...]
            q_groups[i][p * (piece_w // grp_w) + gg] = jnp.concatenate(
                [n0 * cq - n1 * sq, n1 * cq + n0 * sq], axis=1).astype(BF16)

    def z_piece(i, p):
        z = jnp.dot(h_b[i], winb_ref[:, d_model + p * piece_w:d_model + (p + 1) * piece_w],
                    preferred_element_type=F32)
        gates[i][p] = z * jax.nn.sigmoid(z)

    def o_piece(i, p):
        if gated[i] is None:
            full = jnp.concatenate([jnp.concatenate(attn[i][qb], axis=1) for qb in range(n_qb)], axis=0)
            gated[i] = (full * jnp.concatenate(gates[i], axis=1)).astype(BF16)
        rows = slice(i * CHUNK, (i + 1) * CHUNK)
        cols = slice(p * piece_w, (p + 1) * piece_w)
        o_ref[rows, cols] = x_ref[rows, cols] + jnp.dot(gated[i], woutb_ref[:, cols], preferred_element_type=F32)

    def scores(i, t):
        qb, g = divmod(t, N_KV_HEADS)
        k_lo = i * CHUNK + qb * BLOCK
        qg = q_groups[i][g][qb * BLOCK:(qb + 1) * BLOCK, :]
        q4 = jnp.concatenate([jnp.where(quarter_g[hh], qg, jnp.zeros_like(qg)) for hh in range(GROUP)], axis=0)
        return jnp.dot(q4, kt_scr[g, :, k_lo:k_lo + 2 * BLOCK], preferred_element_type=F32)

    def attend(i, t, s):
        qb, g = divmod(t, N_KV_HEADS)
        k_lo = i * CHUNK + qb * BLOCK
        if i == 0 and qb == 0:
            bias = bias_ref[jnp.where(first_tile, 0, 1)]
        else:
            bias = bias_ref[1]
        probs = []
        row_max = []
        for hh in range(GROUP):
            sm = jnp.minimum(s[hh * BLOCK:(hh + 1) * BLOCK, :], bias)
            m = jnp.max(sm, axis=-1, keepdims=True)
            probs.append(jnp.exp2(sm - m).astype(BF16))
            row_max.append(m)
        o4 = jnp.dot(jnp.concatenate(probs, axis=0), v_scr[g, k_lo:k_lo + 2 * BLOCK, :],
                     preferred_element_type=F32)
        for pair in range(GROUP // 2):
            h0, h1 = 2 * pair, 2 * pair + 1
            sink = jnp.where(lo_row, sink_ref[g * GROUP + h0] * LOG2E, sink_ref[g * GROUP + h1] * LOG2E)
            blk0 = o4[h0 * BLOCK:(h0 + 1) * BLOCK, :]
            blk1 = o4[h1 * BLOCK:(h1 + 1) * BLOCK, :]
            den = (jnp.where(lo_b, blk0[:, LANES:2 * LANES], blk1[:, LANES:2 * LANES])
                   + jnp.exp2(sink - jnp.where(lo_b, row_max[h0], row_max[h1])))
            attn[i][qb][2 * g + pair] = jnp.where(lo_b, blk0[:, 0:LANES], blk1[:, 0:LANES]) * (1.0 / den)

    prep(0)
    k_piece(0)
    v_piece(0)
    for p in range(n_pieces):
        q_piece(0, p)
    for i in range(n_chunks):
        fillers = []
        if i > 0:
            fillers += [functools.partial(o_piece, i - 1, p) for p in range(n_pieces)]
        fillers += [functools.partial(z_piece, i, p) for p in range(n_pieces)]
        if i + 1 < n_chunks:
            prep(i + 1)
            fillers += [functools.partial(k_piece, i + 1), functools.partial(v_piece, i + 1)]
            fillers += [functools.partial(q_piece, i + 1, p) for p in range(n_pieces)]
        after = _spread(n_iter, len(fillers))
        pending = [scores(i, t) for t in range(SCORE_LOOKAHEAD)]
        for t in range(n_iter):
            if t + SCORE_LOOKAHEAD < n_iter:
                pending.append(scores(i, t + SCORE_LOOKAHEAD))
            for k, slot in enumerate(after):
                if slot == t:
                    fillers[k]()
            attend(i, t, pending.pop(0))
    for p in range(n_pieces):
        o_piece(n_chunks - 1, p)


def _rope_tables(seq):
    pos = np.arange(seq, dtype=np.float32)
    inv = np.float32(ROPE_THETA) ** (-np.arange(0, ROT_DIM, 2, dtype=np.float32) / np.float32(ROT_DIM))
    ang = pos[:, None] * inv[None, :]
    return np.cos(ang), np.sin(ang)


def _dim_sets(t):
    mid = ROT_DIM + HALF_DIM - ROT_HALF
    set_a = jnp.concatenate([t[..., 0:ROT_HALF], t[..., ROT_DIM:mid]], axis=-1)
    set_b = jnp.concatenate([t[..., ROT_HALF:ROT_DIM], t[..., mid:HEAD_DIM]], axis=-1)
    return set_a, set_b


def _group_permutation():
    perm = np.zeros((GROUP * HEAD_DIM, GROUP * HEAD_DIM), np.float32)
    for si, dims in enumerate((DIMS_A, DIMS_B)):
        for hh in range(GROUP):
            new = si * GROUP * HALF_DIM + hh * HALF_DIM + np.arange(HALF_DIM)
            perm[hh * HEAD_DIM + dims, new] = 1.0
    return perm.astype(BF16)


def _layer_b(x, ln_kv, w_kv, k_norm, ln_b, w_in_b, q_norm, sinks, w_out_b):
    bsz, seq, d = x.shape
    kv2 = w_kv.shape[1]
    grp_w = GROUP * HEAD_DIM
    scale = LOG2E / math.sqrt(HEAD_DIM)

    cos, sin = _rope_tables(seq)
    cos_t = np.ascontiguousarray(cos.T.reshape(ROT_HALF, seq // CHUNK, CHUNK).transpose(1, 0, 2))
    sin_t = np.ascontiguousarray(sin.T.reshape(ROT_HALF, seq // CHUNK, CHUNK).transpose(1, 0, 2))
    jj = np.arange(LANES) % HALF_DIM
    rot = jj < ROT_HALF
    cq = (np.where(rot, cos[:, jj % ROT_HALF], 1.0) * scale).astype(np.float32)
    sq = (np.where(rot, sin[:, jj % ROT_HALF], 0.0) * scale).astype(np.float32)
    gain_a, gain_b = _dim_sets(q_norm)
    qg_a = jnp.tile(gain_a, GROUP).reshape(1, LANES)
    qg_b = jnp.tile(gain_b, GROUP).reshape(1, LANES)
    perm = _group_permutation()

    a_idx = np.arange(BLOCK)[:, None]
    c_idx = np.arange(2 * BLOCK)[None, :]
    rel_ok = (c_idx <= a_idx + BLOCK) & (c_idx > a_idx + BLOCK - WINDOW)
    big = np.finfo(np.float32).max
    bias_rest = np.where(rel_ok, big, NEG_INF).astype(np.float32)
    bias_first = np.where(rel_ok & (c_idx >= BLOCK), big, NEG_INF).astype(np.float32)
    bias = np.stack([bias_first, bias_rest])

    const2 = lambda b, j: (0, 0)
    const3 = lambda b, j: (0, 0, 0)
    single = pl.Buffered(1)
    return pl.pallas_call(
        _layer_b_kernel,
        out_shape=jax.ShapeDtypeStruct(x.shape, x.dtype),
        grid=(bsz, seq // SEQ_TILE),
        in_specs=[
            pl.BlockSpec((None, SEQ_TILE, d), lambda b, j: (b, j, 0)),
            pl.BlockSpec((1, d), const2),
            pl.BlockSpec((1, d), const2),
            pl.BlockSpec((d, kv2), const2, pipeline_mode=single),
            pl.BlockSpec((HEAD_DIM, 1), const2),
            pl.BlockSpec((SEQ_TILE // CHUNK, ROT_HALF, CHUNK), lambda b, j: (j, 0, 0)),
            pl.BlockSpec((SEQ_TILE // CHUNK, ROT_HALF, CHUNK), lambda b, j: (j, 0, 0)),
            pl.BlockSpec((d, 2 * d), const2, pipeline_mode=single),
            pl.BlockSpec((grp_w, grp_w), const2),
            pl.BlockSpec((1, LANES), const2),
            pl.BlockSpec((1, LANES), const2),
            pl.BlockSpec((SEQ_TILE, LANES), lambda b, j: (j, 0)),
            pl.BlockSpec((SEQ_TILE, LANES), lambda b, j: (j, 0)),
            pl.BlockSpec(memory_space=pltpu.SMEM),
            pl.BlockSpec((2, BLOCK, 2 * BLOCK), const3),
            pl.BlockSpec((d, d), const2, pipeline_mode=single),
        ],
        out_specs=pl.BlockSpec((None, SEQ_TILE, d), lambda b, j: (b, j, 0)),
        scratch_shapes=[
            pltpu.VMEM((N_KV_HEADS, grp_w, BLOCK + SEQ_TILE), BF16),
            pltpu.VMEM((N_KV_HEADS, BLOCK + SEQ_TILE, grp_w), BF16),
            pltpu.VMEM((d, d), BF16),
        ],
        compiler_params=pltpu.CompilerParams(
            dimension_semantics=("arbitrary", "arbitrary"),
            vmem_limit_bytes=VMEM_LIMIT_BYTES),
        name="layer_b_swa",
    )(x, ln_kv.reshape(1, d), ln_b.reshape(1, d), w_kv.astype(BF16), k_norm.reshape(HEAD_DIM, 1),
      cos_t, sin_t, w_in_b.astype(BF16), perm, qg_a, qg_b, cq, sq, sinks.astype(F32), bias, w_out_b.astype(BF16))


def kernel(x, ln_a, w_in_a, conv_a, w_out_a, ln_kv, w_kv, k_norm, ln_b, w_in_b, q_norm, sinks, w_out_b):
    n_a = ln_a.shape[0]
    n_b = ln_b.shape[0]
    assert n_b == 1, "the shared K/V is recomputed inside the single attention layer"
    for i in range(n_a):
        x = _layer_a(x, ln_a[i], w_in_a[i], conv_a[i], w_out_a[i])
    for i in range(n_b):
        x = _layer_b(x, ln_kv, w_kv, k_norm, ln_b[i], w_in_b[i], q_norm[i], sinks[i], w_out_b[i])
    return x
```

```python
import functools
import math

import numpy as np

import jax
import jax.numpy as jnp
from jax import lax
from jax.experimental import pallas as pl
from jax.experimental.pallas import tpu as pltpu

F32 = jnp.float32
BF16 = jnp.bfloat16

HEAD_DIM = 64
N_KV_HEADS = 4
GROUP = 4
N_HEADS = N_KV_HEADS * GROUP
WINDOW = 128
BLOCK = 128
ROT_DIM = HEAD_DIM // 4
ROT_HALF = ROT_DIM // 2
ROPE_THETA = 500000.0
EPS = 1e-6
NEG_INF = -1e30
LOG2E = math.log2(math.e)
CONV_W = 3

LANES = 128
SUBLANES = 8
VMEM_LIMIT_BYTES = 56 * 1024 * 1024

SEQ_TILE = 1024
CHUNK = 512
SCORE_LOOKAHEAD = 2
DENSE_PIECE = 512

HALF_DIM = HEAD_DIM // 2
DIMS_A = np.concatenate([np.arange(0, ROT_HALF), np.arange(ROT_DIM, ROT_DIM + HALF_DIM - ROT_HALF)])
DIMS_B = np.concatenate([np.arange(ROT_HALF, ROT_DIM), np.arange(ROT_DIM + HALF_DIM - ROT_HALF, HEAD_DIM)])


def _rms_scale(x):
    return lax.rsqrt(jnp.mean(x * x, axis=-1, keepdims=True) + EPS)


def _layer_a_kernel(*refs, width, n_cast):
    x_ref, g_ref, win_ref, cw_ref, wout_ref = refs[:5]
    cast_in = refs[5:5 + n_cast]
    o_ref = refs[5 + n_cast]
    cast_out = refs[6 + n_cast:6 + 2 * n_cast]
    vbuf = refs[6 + 2 * n_cast]
    w = width
    tile = x_ref.shape[0]
    n_chunks = tile // CHUNK

    @pl.when(pl.program_id(1) == 0)
    def _():
        vbuf[0:SUBLANES, :] = jnp.zeros((SUBLANES, w), F32)

    @pl.when(pl.program_id(1) != 0)
    def _():
        vbuf[0:SUBLANES, :] = vbuf[tile:tile + SUBLANES, :]

    ys = []
    for i in range(n_chunks):
        r0 = i * CHUNK
        v0 = SUBLANES + r0
        x = x_ref[r0:r0 + CHUNK, :]
        h = (x * _rms_scale(x) * g_ref[...]).astype(BF16)
        cu = jnp.dot(h, win_ref[:, w:3 * w], preferred_element_type=F32)
        vbuf[v0:v0 + CHUNK, :] = cu[:, 0:w] * cu[:, w:2 * w]
        conv = (cw_ref[2] * vbuf[v0:v0 + CHUNK, :]
                + cw_ref[1] * vbuf[v0 - 1:v0 - 1 + CHUNK, :]
                + cw_ref[0] * vbuf[v0 - 2:v0 - 2 + CHUNK, :])
        z = jnp.dot(h, win_ref[:, 3 * w:4 * w], preferred_element_type=F32)
        gate = conv * (z * jax.nn.sigmoid(z))
        b_gate = jnp.dot(h, win_ref[:, 0:w], preferred_element_type=F32)
        ys.append((b_gate * gate).astype(BF16))
    for i in range(n_chunks):
        r0 = i * CHUNK
        o_ref[r0:r0 + CHUNK, :] = x_ref[r0:r0 + CHUNK, :] + jnp.dot(
            ys[i], wout_ref[...], preferred_element_type=F32)
    for src, dst in zip(cast_in, cast_out):
        dst[...] = src[...].astype(BF16)


def _layer_a(x, ln_g, w_in, conv_w, w_out, cast_weights=()):
    bsz, seq, d = x.shape
    w = w_out.shape[0]
    n_j = seq // SEQ_TILE
    n_steps = bsz * n_j
    const = lambda b, j: (0, 0)
    slab = lambda b, j: (b * n_j + j, 0)
    single = pl.Buffered(1)
    cast_specs = [pl.BlockSpec((cw.shape[0] // n_steps, cw.shape[1]), slab) for cw in cast_weights]
    outs = pl.pallas_call(
        functools.partial(_layer_a_kernel, width=w, n_cast=len(cast_weights)),
        out_shape=[jax.ShapeDtypeStruct(x.shape, x.dtype)]
        + [jax.ShapeDtypeStruct(cw.shape, BF16) for cw in cast_weights],
        grid=(bsz, n_j),
        in_specs=[
            pl.BlockSpec((None, SEQ_TILE, d), lambda b, j: (b, j, 0)),
            pl.BlockSpec((1, d), const),
            pl.BlockSpec((d, 4 * w), const, pipeline_mode=single),
            pl.BlockSpec((CONV_W, 1, w), lambda b, j: (0, 0, 0)),
            pl.BlockSpec((w, d), const, pipeline_mode=single),
        ] + cast_specs,
        out_specs=[pl.BlockSpec((None, SEQ_TILE, d), lambda b, j: (b, j, 0))] + cast_specs,
        scratch_shapes=[pltpu.VMEM((SUBLANES + SEQ_TILE, w), F32)],
        compiler_params=pltpu.CompilerParams(
            dimension_semantics=("arbitrary", "arbitrary"),
            vmem_limit_bytes=VMEM_LIMIT_BYTES),
        name="layer_a_shortconv",
    )(x, ln_g.reshape(1, d), w_in.astype(BF16), conv_w, w_out.astype(BF16), *cast_weights)
    return outs[0], tuple(outs[1:])


def _spread(n_slots, n_items):
    return [((k + 1) * n_slots) // (n_items + 1) for k in range(n_items)]


def _layer_b_kernel(x_ref, lnkv_ref, lnb_ref, wkv_ref, kg_ref, cost_ref, sint_ref,
                    winb_ref, perm_ref, qga_ref, qgb_ref, cq_ref, sq_ref, sink_ref, bias_ref, woutb_ref,
                    o_ref, kt_scr, v_scr, wq_scr):
    tile, d_model = x_ref.shape
    kv_w = N_KV_HEADS * HEAD_DIM
    grp_w = GROUP * HEAD_DIM
    first_tile = pl.program_id(1) == 0

    @pl.when(jnp.logical_and(pl.program_id(0) == 0, first_tile))
    def _():
        for g in range(N_KV_HEADS):
            cols = slice(g * grp_w, (g + 1) * grp_w)
            wq_scr[:, cols] = jnp.dot(winb_ref[:, cols], perm_ref[...], preferred_element_type=F32).astype(BF16)

    @pl.when(first_tile)
    def _():
        kt_scr[:, :, 0:BLOCK] = jnp.zeros((N_KV_HEADS, grp_w, BLOCK), BF16)
        v_scr[:, 0:BLOCK, :] = jnp.zeros((N_KV_HEADS, BLOCK, grp_w), BF16)

    @pl.when(jnp.logical_not(first_tile))
    def _():
        kt_scr[:, :, 0:BLOCK] = kt_scr[:, :, tile:tile + BLOCK]
        v_scr[:, 0:BLOCK, :] = v_scr[:, tile:tile + BLOCK, :]

    lane_c = lax.broadcasted_iota(jnp.int32, (CHUNK, LANES), 1)
    lo_c = lane_c < HEAD_DIM
    quarter_c = [jnp.logical_and(lane_c >= hh * HALF_DIM, lane_c < (hh + 1) * HALF_DIM) for hh in range(GROUP)]
    lane_g = lax.broadcasted_iota(jnp.int32, (BLOCK, grp_w), 1) % LANES
    quarter_g = [jnp.logical_and(lane_g >= hh * HALF_DIM, lane_g < (hh + 1) * HALF_DIM) for hh in range(GROUP)]
    lo_b = lax.broadcasted_iota(jnp.int32, (BLOCK, LANES), 1) < HEAD_DIM
    lo_row = lax.broadcasted_iota(jnp.int32, (1, LANES), 1) < HEAD_DIM
    ones = jnp.ones((CHUNK, LANES), BF16)

    piece_w = DENSE_PIECE
    n_pieces = d_model // piece_w
    n_chunks = tile // CHUNK
    n_qb = CHUNK // BLOCK
    n_iter = n_qb * N_KV_HEADS
    h_kv = [None] * n_chunks
    h_b = [None] * n_chunks
    q_groups = [[None] * N_KV_HEADS for _ in range(n_chunks)]
    gates = [[None] * n_pieces for _ in range(n_chunks)]
    attn = [[[None] * (2 * N_KV_HEADS) for _ in range(n_qb)] for _ in range(n_chunks)]
    gated = [None] * n_chunks

    def prep(i):
        x = x_ref[i * CHUNK:(i + 1) * CHUNK, :]
        xn = x * _rms_scale(x)
        h_kv[i] = (xn * lnkv_ref[...]).astype(BF16)
        h_b[i] = (xn * lnb_ref[...]).astype(BF16)

    def kv_piece(i):
        k0 = BLOCK + i * CHUNK
        kv = jnp.dot(h_kv[i], wkv_ref[...], preferred_element_type=F32)
        k_t = kv[:, 0:kv_w].T
        cos_t = cost_ref[i]
        sin_t = sint_ref[i]
        for g in range(N_KV_HEADS):
            kh = k_t[g * HEAD_DIM:(g + 1) * HEAD_DIM, :]
            kh = kh * lax.rsqrt(jnp.mean(kh * kh, axis=0, keepdims=True) + EPS) * kg_ref[...]
            r1 = kh[0:ROT_HALF, :]
            r2 = kh[ROT_HALF:ROT_DIM, :]
            set_a = jnp.concatenate([r1 * cos_t - r2 * sin_t, kh[ROT_DIM:ROT_DIM + HALF_DIM - ROT_HALF, :]], axis=0)
            set_b = jnp.concatenate([r2 * cos_t + r1 * sin_t, kh[ROT_DIM + HALF_DIM - ROT_HALF:, :]], axis=0)
            kt_scr[g, :, k0:k0 + CHUNK] = jnp.concatenate([set_a] * GROUP + [set_b] * GROUP, axis=0).astype(BF16)
        for c in range(kv_w // LANES):
            vc = kv[:, kv_w + c * LANES:kv_w + (c + 1) * LANES]
            vr = pltpu.roll(vc, HEAD_DIM, axis=1)
            v_scr[2 * c, k0:k0 + CHUNK, 0:LANES] = jnp.where(lo_c, vc, vr).astype(BF16)
            v_scr[2 * c + 1, k0:k0 + CHUNK, 0:LANES] = jnp.where(lo_c, vr, vc).astype(BF16)
            v_scr[2 * c, k0:k0 + CHUNK, LANES:2 * LANES] = ones
            v_scr[2 * c + 1, k0:k0 + CHUNK, LANES:2 * LANES] = ones

    def q_piece(i, p):
        q = jnp.dot(h_b[i], wq_scr[:, p * piece_w:(p + 1) * piece_w], preferred_element_type=F32)
        cq = cq_ref[i * CHUNK:(i + 1) * CHUNK, :]
        sq = sq_ref[i * CHUNK:(i + 1) * CHUNK, :]
        for gg in range(piece_w // grp_w):
            c0 = q[:, gg * grp_w:gg * grp_w + LANES]
            c1 = q[:, gg * grp_w + LANES:(gg + 1) * grp_w]
            t = c0 * c0 + c1 * c1
            ss = [jnp.sum(jnp.where(quarter_c[hh], t, 0.0), axis=-1, keepdims=True) for hh in range(GROUP)]
            ms = jnp.where(quarter_c[0], ss[0], jnp.where(quarter_c[1], ss[1], jnp.where(quarter_c[2], ss[2], ss[3])))
            rstd = lax.rsqrt(ms * (1.0 / HEAD_DIM) + EPS)
            n0 = c0 * rstd * qga_ref[...]
            n1 = c1 * rstd * qgb_ref[...]
            q_groups[i][p * (piece_w // grp_w) + gg] = jnp.concatenate(
                [n0 * cq - n1 * sq, n1 * cq + n0 * sq], axis=1).astype(BF16)

    def z_piece(i, p):
        z = jnp.dot(h_b[i], winb_ref[:, d_model + p * piece_w:d_model + (p + 1) * piece_w],
                    preferred_element_type=F32)
        gates[i][p] = z * jax.nn.sigmoid(z)

    def o_piece(i, p):
        if gated[i] is None:
            full = jnp.concatenate([jnp.concatenate(attn[i][qb], axis=1) for qb in range(n_qb)], axis=0)
            gated[i] = (full * jnp.concatenate(gates[i], axis=1)).astype(BF16)
        rows = slice(i * CHUNK, (i + 1) * CHUNK)
        cols = slice(p * piece_w, (p + 1) * piece_w)
        o_ref[rows, cols] = x_ref[rows, cols] + jnp.dot(gated[i], woutb_ref[:, cols], preferred_element_type=F32)

    def scores(i, t):
        qb, g = divmod(t, N_KV_HEADS)
        k_lo = i * CHUNK + qb * BLOCK
        qg = q_groups[i][g][qb * BLOCK:(qb + 1) * BLOCK, :]
        q4 = jnp.concatenate([jnp.where(quarter_g[hh], qg, jnp.zeros_like(qg)) for hh in range(GROUP)], axis=0)
        return jnp.dot(q4, kt_scr[g, :, k_lo:k_lo + 2 * BLOCK], preferred_element_type=F32)

    def attend(i, t, s):
        qb, g = divmod(t, N_KV_HEADS)
        k_lo = i * CHUNK + qb * BLOCK
        if i == 0 and qb == 0:
            bias = bias_ref[jnp.where(first_tile, 0, 1)]
        else:
            bias = bias_ref[1]
        probs = []
        row_max = []
        for hh in range(GROUP):
            sm = jnp.minimum(s[hh * BLOCK:(hh + 1) * BLOCK, :], bias)
            m = jnp.max(sm, axis=-1, keepdims=True)
            probs.append(jnp.exp2(sm - m).astype(BF16))
            row_max.append(m)
        o4 = jnp.dot(jnp.concatenate(probs, axis=0), v_scr[g, k_lo:k_lo + 2 * BLOCK, :],
                     preferred_element_type=F32)
        for pair in range(GROUP // 2):
            h0, h1 = 2 * pair, 2 * pair + 1
            sink = jnp.where(lo_row, sink_ref[g * GROUP + h0] * LOG2E, sink_ref[g * GROUP + h1] * LOG2E)
            blk0 = o4[h0 * BLOCK:(h0 + 1) * BLOCK, :]
            blk1 = o4[h1 * BLOCK:(h1 + 1) * BLOCK, :]
            den = (jnp.where(lo_b, blk0[:, LANES:2 * LANES], blk1[:, LANES:2 * LANES])
                   + jnp.exp2(sink - jnp.where(lo_b, row_max[h0], row_max[h1])))
            attn[i][qb][2 * g + pair] = jnp.where(lo_b, blk0[:, 0:LANES], blk1[:, 0:LANES]) * (1.0 / den)

    prep(0)
    kv_piece(0)
    for p in range(n_pieces):
        q_piece(0, p)
    for i in range(n_chunks):
        fillers = []
        if i > 0:
            fillers += [functools.partial(o_piece, i - 1, p) for p in range(n_pieces)]
        fillers += [functools.partial(z_piece, i, p) for p in range(n_pieces)]
        if i + 1 < n_chunks:
            prep(i + 1)
            fillers += [functools.partial(kv_piece, i + 1)]
            fillers += [functools.partial(q_piece, i + 1, p) for p in range(n_pieces)]
        after = _spread(n_iter, len(fillers))
        pending = [scores(i, t) for t in range(SCORE_LOOKAHEAD)]
        for t in range(n_iter):
            if t + SCORE_LOOKAHEAD < n_iter:
                pending.append(scores(i, t + SCORE_LOOKAHEAD))
            for k, slot in enumerate(after):
                if slot == t:
                    fillers[k]()
            attend(i, t, pending.pop(0))
    for p in range(n_pieces):
        o_piece(n_chunks - 1, p)


def _rope_tables(seq):
    pos = np.arange(seq, dtype=np.float32)
    inv = np.float32(ROPE_THETA) ** (-np.arange(0, ROT_DIM, 2, dtype=np.float32) / np.float32(ROT_DIM))
    ang = pos[:, None] * inv[None, :]
    return np.cos(ang), np.sin(ang)


def _dim_sets(t):
    mid = ROT_DIM + HALF_DIM - ROT_HALF
    set_a = jnp.concatenate([t[..., 0:ROT_HALF], t[..., ROT_DIM:mid]], axis=-1)
    set_b = jnp.concatenate([t[..., ROT_HALF:ROT_DIM], t[..., mid:HEAD_DIM]], axis=-1)
    return set_a, set_b


def _group_permutation():
    perm = np.zeros((GROUP * HEAD_DIM, GROUP * HEAD_DIM), np.float32)
    for si, dims in enumerate((DIMS_A, DIMS_B)):
        for hh in range(GROUP):
            new = si * GROUP * HALF_DIM + hh * HALF_DIM + np.arange(HALF_DIM)
            perm[hh * HEAD_DIM + dims, new] = 1.0
    return perm.astype(BF16)


def _layer_b(x, ln_kv, w_kv, k_norm, ln_b, w_in_b, q_norm, sinks, w_out_b):
    bsz, seq, d = x.shape
    kv2 = w_kv.shape[1]
    grp_w = GROUP * HEAD_DIM
    scale = LOG2E / math.sqrt(HEAD_DIM)

    cos, sin = _rope_tables(seq)
    cos_t = np.ascontiguousarray(cos.T.reshape(ROT_HALF, seq // CHUNK, CHUNK).transpose(1, 0, 2))
    sin_t = np.ascontiguousarray(sin.T.reshape(ROT_HALF, seq // CHUNK, CHUNK).transpose(1, 0, 2))
    jj = np.arange(LANES) % HALF_DIM
    rot = jj < ROT_HALF
    cq = (np.where(rot, cos[:, jj % ROT_HALF], 1.0) * scale).astype(np.float32)
    sq = (np.where(rot, sin[:, jj % ROT_HALF], 0.0) * scale).astype(np.float32)
    gain_a, gain_b = _dim_sets(q_norm)
    qg_a = jnp.tile(gain_a, GROUP).reshape(1, LANES)
    qg_b = jnp.tile(gain_b, GROUP).reshape(1, LANES)
    perm = _group_permutation()

    a_idx = np.arange(BLOCK)[:, None]
    c_idx = np.arange(2 * BLOCK)[None, :]
    rel_ok = (c_idx <= a_idx + BLOCK) & (c_idx > a_idx + BLOCK - WINDOW)
    big = np.finfo(np.float32).max
    bias_rest = np.where(rel_ok, big, NEG_INF).astype(np.float32)
    bias_first = np.where(rel_ok & (c_idx >= BLOCK), big, NEG_INF).astype(np.float32)
    bias = np.stack([bias_first, bias_rest])

    const2 = lambda b, j: (0, 0)
    const3 = lambda b, j: (0, 0, 0)
    single = pl.Buffered(1)
    return pl.pallas_call(
        _layer_b_kernel,
        out_shape=jax.ShapeDtypeStruct(x.shape, x.dtype),
        grid=(bsz, seq // SEQ_TILE),
        in_specs=[
            pl.BlockSpec((None, SEQ_TILE, d), lambda b, j: (b, j, 0)),
            pl.BlockSpec((1, d), const2),
            pl.BlockSpec((1, d), const2),
            pl.BlockSpec((d, kv2), const2, pipeline_mode=single),
            pl.BlockSpec((HEAD_DIM, 1), const2),
            pl.BlockSpec((SEQ_TILE // CHUNK, ROT_HALF, CHUNK), lambda b, j: (j, 0, 0)),
            pl.BlockSpec((SEQ_TILE // CHUNK, ROT_HALF, CHUNK), lambda b, j: (j, 0, 0)),
            pl.BlockSpec((d, 2 * d), const2, pipeline_mode=single),
            pl.BlockSpec((grp_w, grp_w), const2),
            pl.BlockSpec((1, LANES), const2),
            pl.BlockSpec((1, LANES), const2),
            pl.BlockSpec((SEQ_TILE, LANES), lambda b, j: (j, 0)),
            pl.BlockSpec((SEQ_TILE, LANES), lambda b, j: (j, 0)),
            pl.BlockSpec(memory_space=pltpu.SMEM),
            pl.BlockSpec((2, BLOCK, 2 * BLOCK), const3),
            pl.BlockSpec((d, d), const2, pipeline_mode=single),
        ],
        out_specs=pl.BlockSpec((None, SEQ_TILE, d), lambda b, j: (b, j, 0)),
        scratch_shapes=[
            pltpu.VMEM((N_KV_HEADS, grp_w, BLOCK + SEQ_TILE), BF16),
            pltpu.VMEM((N_KV_HEADS, BLOCK + SEQ_TILE, grp_w), BF16),
            pltpu.VMEM((d, d), BF16),
        ],
        compiler_params=pltpu.CompilerParams(
            dimension_semantics=("arbitrary", "arbitrary"),
            vmem_limit_bytes=VMEM_LIMIT_BYTES),
        name="layer_b_swa",
    )(x, ln_kv.reshape(1, d), ln_b.reshape(1, d), w_kv.astype(BF16), k_norm.reshape(HEAD_DIM, 1),
      cos_t, sin_t, w_in_b.astype(BF16), perm, qg_a, qg_b, cq, sq, sinks.astype(F32), bias, w_out_b.astype(BF16))


def kernel(x, ln_a, w_in_a, conv_a, w_out_a, ln_kv, w_kv, k_norm, ln_b, w_in_b, q_norm, sinks, w_out_b):
    n_a = ln_a.shape[0]
    n_b = ln_b.shape[0]
    assert n_b == 1, "the shared K/V is recomputed inside the single attention layer"
    assert n_a >= 1
    for i in range(n_a):
        nxt = (w_kv, w_in_b[0], w_out_b[0]) if i == n_a - 1 else ()
        x, cast = _layer_a(x, ln_a[i], w_in_a[i], conv_a[i], w_out_a[i], nxt)
    w_kv16, w_in_b16, w_out_b16 = cast
    return _layer_b(x, ln_kv, w_kv16, k_norm, ln_b[0], w_in_b16, q_norm[0], sinks[0], w_out_b16)
```

```python
import functools
import math

import numpy as np

import jax
import jax.numpy as jnp
from jax import lax
from jax.experimental import pallas as pl
from jax.experimental.pallas import tpu as pltpu

F32 = jnp.float32
BF16 = jnp.bfloat16

HEAD_DIM = 64
N_KV_HEADS = 4
GROUP = 4
N_HEADS = N_KV_HEADS * GROUP
WINDOW = 128
BLOCK = 128
assert WINDOW == BLOCK
ROT_DIM = HEAD_DIM // 4
ROT_HALF = ROT_DIM // 2
ROPE_THETA = 500000.0
EPS = 1e-6
NEG_INF = -1e30
FLT_MAX = float(np.finfo(np.float32).max)
LOG2E = math.log2(math.e)
CONV_W = 3

LANES = 128
SUBLANES = 8
VMEM_LIMIT_BYTES = 56 * 1024 * 1024

SEQ_TILE = 1024
CHUNK = 512
SCORE_LOOKAHEAD = 2
DENSE_PIECE = 512

HALF_DIM = HEAD_DIM // 2
DIMS_A = np.concatenate([np.arange(0, ROT_HALF), np.arange(ROT_DIM, ROT_DIM + HALF_DIM - ROT_HALF)])
DIMS_B = np.concatenate([np.arange(ROT_HALF, ROT_DIM), np.arange(ROT_DIM + HALF_DIM - ROT_HALF, HEAD_DIM)])


def _rms_scale(x):
    return lax.rsqrt(jnp.mean(x * x, axis=-1, keepdims=True) + EPS)


def _layer_a_kernel(*refs, width, n_cast):
    x_ref, g_ref, win_ref, cw_ref, wout_ref = refs[:5]
    cast_in = refs[5:5 + n_cast]
    o_ref = refs[5 + n_cast]
    cast_out = refs[6 + n_cast:6 + 2 * n_cast]
    vbuf = refs[6 + 2 * n_cast]
    w = width
    tile = x_ref.shape[0]
    n_chunks = tile // CHUNK

    @pl.when(pl.program_id(1) == 0)
    def _():
        vbuf[0:SUBLANES, :] = jnp.zeros((SUBLANES, w), F32)

    @pl.when(pl.program_id(1) != 0)
    def _():
        vbuf[0:SUBLANES, :] = vbuf[tile:tile + SUBLANES, :]

    ys = []
    for i in range(n_chunks):
        r0 = i * CHUNK
        v0 = SUBLANES + r0
        x = x_ref[r0:r0 + CHUNK, :]
        h = (x * _rms_scale(x) * g_ref[...]).astype(BF16)
        cu = jnp.dot(h, win_ref[:, w:3 * w], preferred_element_type=F32)
        vbuf[v0:v0 + CHUNK, :] = cu[:, 0:w] * cu[:, w:2 * w]
        conv = (cw_ref[2] * vbuf[v0:v0 + CHUNK, :]
                + cw_ref[1] * vbuf[v0 - 1:v0 - 1 + CHUNK, :]
                + cw_ref[0] * vbuf[v0 - 2:v0 - 2 + CHUNK, :])
        z = jnp.dot(h, win_ref[:, 3 * w:4 * w], preferred_element_type=F32)
        gate = conv * (z * jax.nn.sigmoid(z))
        b_gate = jnp.dot(h, win_ref[:, 0:w], preferred_element_type=F32)
        ys.append((b_gate * gate).astype(BF16))
    for i in range(n_chunks):
        r0 = i * CHUNK
        o_ref[r0:r0 + CHUNK, :] = x_ref[r0:r0 + CHUNK, :] + jnp.dot(
            ys[i], wout_ref[...], preferred_element_type=F32)
    for src, dst in zip(cast_in, cast_out):
        dst[...] = src[...].astype(BF16)


def _layer_a(x, ln_g, w_in, conv_w, w_out, cast_weights=()):
    bsz, seq, d = x.shape
    w = w_out.shape[0]
    n_j = seq // SEQ_TILE
    n_steps = bsz * n_j
    const = lambda b, j: (0, 0)
    slab = lambda b, j: (b * n_j + j, 0)
    single = pl.Buffered(1)
    cast_specs = [pl.BlockSpec((cw.shape[0] // n_steps, cw.shape[1]), slab) for cw in cast_weights]
    outs = pl.pallas_call(
        functools.partial(_layer_a_kernel, width=w, n_cast=len(cast_weights)),
        out_shape=[jax.ShapeDtypeStruct(x.shape, x.dtype)]
        + [jax.ShapeDtypeStruct(cw.shape, BF16) for cw in cast_weights],
        grid=(bsz, n_j),
        in_specs=[
            pl.BlockSpec((None, SEQ_TILE, d), lambda b, j: (b, j, 0)),
            pl.BlockSpec((1, d), const),
            pl.BlockSpec((d, 4 * w), const, pipeline_mode=single),
            pl.BlockSpec((CONV_W, 1, w), lambda b, j: (0, 0, 0)),
            pl.BlockSpec((w, d), const, pipeline_mode=single),
        ] + cast_specs,
        out_specs=[pl.BlockSpec((None, SEQ_TILE, d), lambda b, j: (b, j, 0))] + cast_specs,
        scratch_shapes=[pltpu.VMEM((SUBLANES + SEQ_TILE, w), F32)],
        compiler_params=pltpu.CompilerParams(
            dimension_semantics=("arbitrary", "arbitrary"),
            vmem_limit_bytes=VMEM_LIMIT_BYTES),
        name="layer_a_shortconv",
    )(x, ln_g.reshape(1, d), w_in.astype(BF16), conv_w, w_out.astype(BF16), *cast_weights)
    return outs[0], tuple(outs[1:])


def _spread(n_slots, n_items):
    return [((k + 1) * n_slots) // (n_items + 1) for k in range(n_items)]


def _layer_b_kernel(x_ref, lnkv_ref, lnb_ref, wkv_ref, kg_ref, cost_ref, sint_ref,
                    winb_ref, perm_ref, qga_ref, qgb_ref, cq_ref, sq_ref, sink_ref, woutb_ref,
                    o_ref, kt_scr, v_scr, wq_scr):
    tile, d_model = x_ref.shape
    kv_w = N_KV_HEADS * HEAD_DIM
    grp_w = GROUP * HEAD_DIM
    first_tile = pl.program_id(1) == 0

    @pl.when(jnp.logical_and(pl.program_id(0) == 0, first_tile))
    def _():
        for g in range(N_KV_HEADS):
            cols = slice(g * grp_w, (g + 1) * grp_w)
            wq_scr[:, cols] = jnp.dot(winb_ref[:, cols], perm_ref[...], preferred_element_type=F32).astype(BF16)

    @pl.when(first_tile)
    def _():
        kt_scr[:, :, 0:BLOCK] = jnp.zeros((N_KV_HEADS, grp_w, BLOCK), BF16)
        v_scr[:, 0:BLOCK, :] = jnp.zeros((N_KV_HEADS, BLOCK, grp_w), BF16)

    @pl.when(jnp.logical_not(first_tile))
    def _():
        kt_scr[:, :, 0:BLOCK] = kt_scr[:, :, tile:tile + BLOCK]
        v_scr[:, 0:BLOCK, :] = v_scr[:, tile:tile + BLOCK, :]

    lane_c = lax.broadcasted_iota(jnp.int32, (CHUNK, LANES), 1)
    lo_c = lane_c < HEAD_DIM
    quarter_c = [jnp.logical_and(lane_c >= hh * HALF_DIM, lane_c < (hh + 1) * HALF_DIM) for hh in range(GROUP)]
    lane_g = lax.broadcasted_iota(jnp.int32, (BLOCK, grp_w), 1) % LANES
    quarter_g = [jnp.logical_and(lane_g >= hh * HALF_DIM, lane_g < (hh + 1) * HALF_DIM) for hh in range(GROUP)]
    lo_b = lax.broadcasted_iota(jnp.int32, (BLOCK, LANES), 1) < HEAD_DIM
    lo_row = lax.broadcasted_iota(jnp.int32, (1, LANES), 1) < HEAD_DIM
    prev_valid = (lax.broadcasted_iota(jnp.int32, (BLOCK, BLOCK), 1)
                  > lax.broadcasted_iota(jnp.int32, (BLOCK, BLOCK), 0) + (BLOCK - WINDOW))
    ones = jnp.ones((CHUNK, LANES), BF16)

    piece_w = DENSE_PIECE
    n_pieces = d_model // piece_w
    n_chunks = tile // CHUNK
    n_qb = CHUNK // BLOCK
    n_iter = n_qb * N_KV_HEADS
    h_kv = [None] * n_chunks
    h_b = [None] * n_chunks
    q_groups = [[None] * N_KV_HEADS for _ in range(n_chunks)]
    gates = [[None] * n_pieces for _ in range(n_chunks)]
    attn = [[[None] * (2 * N_KV_HEADS) for _ in range(n_qb)] for _ in range(n_chunks)]
    gated = [None] * n_chunks

    def prep(i):
        x = x_ref[i * CHUNK:(i + 1) * CHUNK, :]
        xn = x * _rms_scale(x)
        h_kv[i] = (xn * lnkv_ref[...]).astype(BF16)
        h_b[i] = (xn * lnb_ref[...]).astype(BF16)

    def kv_piece(i):
        k0 = BLOCK + i * CHUNK
        kv = jnp.dot(h_kv[i], wkv_ref[...], preferred_element_type=F32)
        k_t = kv[:, 0:kv_w].T
        cos_t = cost_ref[i]
        sin_t = sint_ref[i]
        for g in range(N_KV_HEADS):
            kh = k_t[g * HEAD_DIM:(g + 1) * HEAD_DIM, :]
            kh = kh * lax.rsqrt(jnp.mean(kh * kh, axis=0, keepdims=True) + EPS) * kg_ref[...]
            r1 = kh[0:ROT_HALF, :]
            r2 = kh[ROT_HALF:ROT_DIM, :]
            set_a = jnp.concatenate([r1 * cos_t - r2 * sin_t, kh[ROT_DIM:ROT_DIM + HALF_DIM - ROT_HALF, :]], axis=0)
            set_b = jnp.concatenate([r2 * cos_t + r1 * sin_t, kh[ROT_DIM + HALF_DIM - ROT_HALF:, :]], axis=0)
            kt_scr[g, :, k0:k0 + CHUNK] = jnp.concatenate([set_a] * GROUP + [set_b] * GROUP, axis=0).astype(BF16)
        for c in range(kv_w // LANES):
            vc = kv[:, kv_w + c * LANES:kv_w + (c + 1) * LANES]
            vr = pltpu.roll(vc, HEAD_DIM, axis=1)
            v_scr[2 * c, k0:k0 + CHUNK, 0:LANES] = jnp.where(lo_c, vc, vr).astype(BF16)
            v_scr[2 * c + 1, k0:k0 + CHUNK, 0:LANES] = jnp.where(lo_c, vr, vc).astype(BF16)
            v_scr[2 * c, k0:k0 + CHUNK, LANES:2 * LANES] = ones
            v_scr[2 * c + 1, k0:k0 + CHUNK, LANES:2 * LANES] = ones

    def q_piece(i, p):
        q = jnp.dot(h_b[i], wq_scr[:, p * piece_w:(p + 1) * piece_w], preferred_element_type=F32)
        cq = cq_ref[i * CHUNK:(i + 1) * CHUNK, :]
        sq = sq_ref[i * CHUNK:(i + 1) * CHUNK, :]
        for gg in range(piece_w // grp_w):
            c0 = q[:, gg * grp_w:gg * grp_w + LANES]
            c1 = q[:, gg * grp_w + LANES:(gg + 1) * grp_w]
            t = c0 * c0 + c1 * c1
            ss = [jnp.sum(jnp.where(quarter_c[hh], t, 0.0), axis=-1, keepdims=True) for hh in range(GROUP)]
            ms = jnp.where(quarter_c[0], ss[0], jnp.where(quarter_c[1], ss[1], jnp.where(quarter_c[2], ss[2], ss[3])))
            rstd = lax.rsqrt(ms * (1.0 / HEAD_DIM) + EPS)
            n0 = c0 * rstd * qga_ref[...]
            n1 = c1 * rstd * qgb_ref[...]
            q_groups[i][p * (piece_w // grp_w) + gg] = jnp.concatenate(
                [n0 * cq - n1 * sq, n1 * cq + n0 * sq], axis=1).astype(BF16)

    def z_piece(i, p):
        z = jnp.dot(h_b[i], winb_ref[:, d_model + p * piece_w:d_model + (p + 1) * piece_w],
                    preferred_element_type=F32)
        gates[i][p] = z * jax.nn.sigmoid(z)

    def o_piece(i, p):
        if gated[i] is None:
            full = jnp.concatenate([jnp.concatenate(attn[i][qb], axis=1) for qb in range(n_qb)], axis=0)
            gated[i] = (full * jnp.concatenate(gates[i], axis=1)).astype(BF16)
        rows = slice(i * CHUNK, (i + 1) * CHUNK)
        cols = slice(p * piece_w, (p + 1) * piece_w)
        o_ref[rows, cols] = x_ref[rows, cols] + jnp.dot(gated[i], woutb_ref[:, cols], preferred_element_type=F32)

    def scores(i, t):
        qb, g = divmod(t, N_KV_HEADS)
        k_lo = i * CHUNK + qb * BLOCK
        qg = q_groups[i][g][qb * BLOCK:(qb + 1) * BLOCK, :]
        q4 = jnp.concatenate([jnp.where(quarter_g[hh], qg, jnp.zeros_like(qg)) for hh in range(GROUP)], axis=0)
        return jnp.dot(q4, kt_scr[g, :, k_lo:k_lo + 2 * BLOCK], preferred_element_type=F32)

    def attend(i, t, s):
        qb, g = divmod(t, N_KV_HEADS)
        k_lo = i * CHUNK + qb * BLOCK
        probs = []
        row_max = []
        for hh in range(GROUP):
            s_prev = s[hh * BLOCK:(hh + 1) * BLOCK, 0:BLOCK]
            s_own = s[hh * BLOCK:(hh + 1) * BLOCK, BLOCK:2 * BLOCK]
            if i == 0 and qb == 0:
                s_prev = jnp.minimum(s_prev, jnp.where(first_tile, NEG_INF, FLT_MAX))
            sm = jnp.where(prev_valid, s_prev, s_own)
            m = jnp.max(sm, axis=-1, keepdims=True)
            p = jnp.exp2(sm - m).astype(BF16)
            zero = jnp.zeros_like(p)
            probs.append(jnp.concatenate([jnp.where(prev_valid, p, zero), jnp.where(prev_valid, zero, p)], axis=1))
            row_max.append(m)
        o4 = jnp.dot(jnp.concatenate(probs, axis=0), v_scr[g, k_lo:k_lo + 2 * BLOCK, :],
                     preferred_element_type=F32)
        for pair in range(GROUP // 2):
            h0, h1 = 2 * pair, 2 * pair + 1
            sink = jnp.where(lo_row, sink_ref[g * GROUP + h0] * LOG2E, sink_ref[g * GROUP + h1] * LOG2E)
            blk0 = o4[h0 * BLOCK:(h0 + 1) * BLOCK, :]
            blk1 = o4[h1 * BLOCK:(h1 + 1) * BLOCK, :]
            den = (jnp.where(lo_b, blk0[:, LANES:2 * LANES], blk1[:, LANES:2 * LANES])
                   + jnp.exp2(sink - jnp.where(lo_b, row_max[h0], row_max[h1])))
            attn[i][qb][2 * g + pair] = jnp.where(lo_b, blk0[:, 0:LANES], blk1[:, 0:LANES]) * (1.0 / den)

    prep(0)
    kv_piece(0)
    for p in range(n_pieces):
        q_piece(0, p)
    for i in range(n_chunks):
        fillers = []
        if i > 0:
            fillers += [functools.partial(o_piece, i - 1, p) for p in range(n_pieces)]
        fillers += [functools.partial(z_piece, i, p) for p in range(n_pieces)]
        if i + 1 < n_chunks:
            prep(i + 1)
            fillers += [functools.partial(kv_piece, i + 1)]
            fillers += [functools.partial(q_piece, i + 1, p) for p in range(n_pieces)]
        after = _spread(n_iter, len(fillers))
        pending = [scores(i, t) for t in range(SCORE_LOOKAHEAD)]
        for t in range(n_iter):
            if t + SCORE_LOOKAHEAD < n_iter:
                pending.append(scores(i, t + SCORE_LOOKAHEAD))
            for k, slot in enumerate(after):
                if slot == t:
                    fillers[k]()
            attend(i, t, pending.pop(0))
    for p in range(n_pieces):
        o_piece(n_chunks - 1, p)


def _rope_tables(seq):
    pos = np.arange(seq, dtype=np.float32)
    inv = np.float32(ROPE_THETA) ** (-np.arange(0, ROT_DIM, 2, dtype=np.float32) / np.float32(ROT_DIM))
    ang = pos[:, None] * inv[None, :]
    return np.cos(ang), np.sin(ang)


def _dim_sets(t):
    mid = ROT_DIM + HALF_DIM - ROT_HALF
    set_a = jnp.concatenate([t[..., 0:ROT_HALF], t[..., ROT_DIM:mid]], axis=-1)
    set_b = jnp.concatenate([t[..., ROT_HALF:ROT_DIM], t[..., mid:HEAD_DIM]], axis=-1)
    return set_a, set_b


def _group_permutation():
    perm = np.zeros((GROUP * HEAD_DIM, GROUP * HEAD_DIM), np.float32)
    for si, dims in enumerate((DIMS_A, DIMS_B)):
        for hh in range(GROUP):
            new = si * GROUP * HALF_DIM + hh * HALF_DIM + np.arange(HALF_DIM)
            perm[hh * HEAD_DIM + dims, new] = 1.0
    return perm.astype(BF16)


def _layer_b(x, ln_kv, w_kv, k_norm, ln_b, w_in_b, q_norm, sinks, w_out_b):
    bsz, seq, d = x.shape
    kv2 = w_kv.shape[1]
    grp_w = GROUP * HEAD_DIM
    scale = LOG2E / math.sqrt(HEAD_DIM)

    cos, sin = _rope_tables(seq)
    cos_t = np.ascontiguousarray(cos.T.reshape(ROT_HALF, seq // CHUNK, CHUNK).transpose(1, 0, 2))
    sin_t = np.ascontiguousarray(sin.T.reshape(ROT_HALF, seq // CHUNK, CHUNK).transpose(1, 0, 2))
    jj = np.arange(LANES) % HALF_DIM
    rot = jj < ROT_HALF
    cq = (np.where(rot, cos[:, jj % ROT_HALF], 1.0) * scale).astype(np.float32)
    sq = (np.where(rot, sin[:, jj % ROT_HALF], 0.0) * scale).astype(np.float32)
    gain_a, gain_b = _dim_sets(q_norm)
    qg_a = jnp.tile(gain_a, GROUP).reshape(1, LANES)
    qg_b = jnp.tile(gain_b, GROUP).reshape(1, LANES)
    perm = _group_permutation()

    const2 = lambda b, j: (0, 0)
    single = pl.Buffered(1)
    return pl.pallas_call(
        _layer_b_kernel,
        out_shape=jax.ShapeDtypeStruct(x.shape, x.dtype),
        grid=(bsz, seq // SEQ_TILE),
        in_specs=[
            pl.BlockSpec((None, SEQ_TILE, d), lambda b, j: (b, j, 0)),
            pl.BlockSpec((1, d), const2),
            pl.BlockSpec((1, d), const2),
            pl.BlockSpec((d, kv2), const2, pipeline_mode=single),
            pl.BlockSpec((HEAD_DIM, 1), const2),
            pl.BlockSpec((SEQ_TILE // CHUNK, ROT_HALF, CHUNK), lambda b, j: (j, 0, 0)),
            pl.BlockSpec((SEQ_TILE // CHUNK, ROT_HALF, CHUNK), lambda b, j: (j, 0, 0)),
            pl.BlockSpec((d, 2 * d), const2, pipeline_mode=single),
            pl.BlockSpec((grp_w, grp_w), const2),
            pl.BlockSpec((1, LANES), const2),
            pl.BlockSpec((1, LANES), const2),
            pl.BlockSpec((SEQ_TILE, LANES), lambda b, j: (j, 0)),
            pl.BlockSpec((SEQ_TILE, LANES), lambda b, j: (j, 0)),
            pl.BlockSpec(memory_space=pltpu.SMEM),
            pl.BlockSpec((d, d), const2, pipeline_mode=single),
        ],
        out_specs=pl.BlockSpec((None, SEQ_TILE, d), lambda b, j: (b, j, 0)),
        scratch_shapes=[
            pltpu.VMEM((N_KV_HEADS, grp_w, BLOCK + SEQ_TILE), BF16),
            pltpu.VMEM((N_KV_HEADS, BLOCK + SEQ_TILE, grp_w), BF16),
            pltpu.VMEM((d, d), BF16),
        ],
        compiler_params=pltpu.CompilerParams(
            dimension_semantics=("arbitrary", "arbitrary"),
            vmem_limit_bytes=VMEM_LIMIT_BYTES),
        name="layer_b_swa",
    )(x, ln_kv.reshape(1, d), ln_b.reshape(1, d), w_kv.astype(BF16), k_norm.reshape(HEAD_DIM, 1),
      cos_t, sin_t, w_in_b.astype(BF16), perm, qg_a, qg_b, cq, sq, sinks.astype(F32), w_out_b.astype(BF16))


def kernel(x, ln_a, w_in_a, conv_a, w_out_a, ln_kv, w_kv, k_norm, ln_b, w_in_b, q_norm, sinks, w_out_b):
    n_a = ln_a.shape[0]
    n_b = ln_b.shape[0]
    assert n_b == 1, "the shared K/V is recomputed inside the single attention layer"
    assert n_a >= 1
    for i in range(n_a):
        nxt = (w_kv, w_in_b[0], w_out_b[0]) if i == n_a - 1 else ()
        x, cast = _layer_a(x, ln_a[i], w_in_a[i], conv_a[i], w_out_a[i], nxt)
    w_kv16, w_in_b16, w_out_b16 = cast
    return _layer_b(x, ln_kv, w_kv16, k_norm, ln_b[0], w_in_b16, q_norm[0], sinks[0], w_out_b16)
```

```python
import functools
import math

import numpy as np

import jax
import jax.numpy as jnp
from jax import lax
from jax.experimental import pallas as pl
from jax.experimental.pallas import tpu as pltpu

F32 = jnp.float32
BF16 = jnp.bfloat16

HEAD_DIM = 64
N_KV_HEADS = 4
GROUP = 4
N_HEADS = N_KV_HEADS * GROUP
WINDOW = 128
BLOCK = 128
assert WINDOW == BLOCK
ROT_DIM = HEAD_DIM // 4
ROT_HALF = ROT_DIM // 2
ROPE_THETA = 500000.0
EPS = 1e-6
NEG_INF = -1e30
FLT_MAX = float(np.finfo(np.float32).max)
LOG2E = math.log2(math.e)
CONV_W = 3

LANES = 128
SUBLANES = 8
VMEM_LIMIT_BYTES = 56 * 1024 * 1024

SEQ_TILE = 1024
CHUNK = 512
SCORE_LOOKAHEAD = 2
DENSE_PIECE = 512

HALF_DIM = HEAD_DIM // 2
DIMS_A = np.concatenate([np.arange(0, ROT_HALF), np.arange(ROT_DIM, ROT_DIM + HALF_DIM - ROT_HALF)])
DIMS_B = np.concatenate([np.arange(ROT_HALF, ROT_DIM), np.arange(ROT_DIM + HALF_DIM - ROT_HALF, HEAD_DIM)])


def _rms_scale(x):
    return lax.rsqrt(jnp.mean(x * x, axis=-1, keepdims=True) + EPS)


def _layer_a_kernel(*refs, width, n_cast):
    x_ref, g_ref, win_ref, cw_ref, wout_ref = refs[:5]
    cast_in = refs[5:5 + n_cast]
    o_ref = refs[5 + n_cast]
    cast_out = refs[6 + n_cast:6 + 2 * n_cast]
    vbuf = refs[6 + 2 * n_cast]
    w = width
    tile = x_ref.shape[0]
    n_chunks = tile // CHUNK

    @pl.when(pl.program_id(1) == 0)
    def _():
        vbuf[0:SUBLANES, :] = jnp.zeros((SUBLANES, w), F32)

    @pl.when(pl.program_id(1) != 0)
    def _():
        vbuf[0:SUBLANES, :] = vbuf[tile:tile + SUBLANES, :]

    ys = []
    for i in range(n_chunks):
        r0 = i * CHUNK
        v0 = SUBLANES + r0
        x = x_ref[r0:r0 + CHUNK, :]
        h = (x * _rms_scale(x) * g_ref[...]).astype(BF16)
        cu = jnp.dot(h, win_ref[:, w:3 * w].astype(BF16), preferred_element_type=F32)
        vbuf[v0:v0 + CHUNK, :] = cu[:, 0:w] * cu[:, w:2 * w]
        conv = (cw_ref[2] * vbuf[v0:v0 + CHUNK, :]
                + cw_ref[1] * vbuf[v0 - 1:v0 - 1 + CHUNK, :]
                + cw_ref[0] * vbuf[v0 - 2:v0 - 2 + CHUNK, :])
        z = jnp.dot(h, win_ref[:, 3 * w:4 * w].astype(BF16), preferred_element_type=F32)
        gate = conv * (z * jax.nn.sigmoid(z))
        b_gate = jnp.dot(h, win_ref[:, 0:w].astype(BF16), preferred_element_type=F32)
        ys.append((b_gate * gate).astype(BF16))
    for i in range(n_chunks):
        r0 = i * CHUNK
        o_ref[r0:r0 + CHUNK, :] = x_ref[r0:r0 + CHUNK, :] + jnp.dot(
            ys[i], wout_ref[...].astype(BF16), preferred_element_type=F32)
    for src, dst in zip(cast_in, cast_out):
        dst[...] = src[...].astype(BF16)


def _layer_a(x, ln_g, w_in, conv_w, w_out, cast_weights=()):
    bsz, seq, d = x.shape
    w = w_out.shape[0]
    n_j = seq // SEQ_TILE
    n_steps = bsz * n_j
    const = lambda b, j: (0, 0)
    slab = lambda b, j: (b * n_j + j, 0)
    single = pl.Buffered(1)
    cast_specs = [pl.BlockSpec((cw.shape[0] // n_steps, cw.shape[1]), slab) for cw in cast_weights]
    outs = pl.pallas_call(
        functools.partial(_layer_a_kernel, width=w, n_cast=len(cast_weights)),
        out_shape=[jax.ShapeDtypeStruct(x.shape, x.dtype)]
        + [jax.ShapeDtypeStruct(cw.shape, BF16) for cw in cast_weights],
        grid=(bsz, n_j),
        in_specs=[
            pl.BlockSpec((None, SEQ_TILE, d), lambda b, j: (b, j, 0)),
            pl.BlockSpec((1, d), const),
            pl.BlockSpec((d, 4 * w), const, pipeline_mode=single),
            pl.BlockSpec((CONV_W, 1, w), lambda b, j: (0, 0, 0)),
            pl.BlockSpec((w, d), const, pipeline_mode=single),
        ] + cast_specs,
        out_specs=[pl.BlockSpec((None, SEQ_TILE, d), lambda b, j: (b, j, 0))] + cast_specs,
        scratch_shapes=[pltpu.VMEM((SUBLANES + SEQ_TILE, w), F32)],
        compiler_params=pltpu.CompilerParams(
            dimension_semantics=("arbitrary", "arbitrary"),
            vmem_limit_bytes=VMEM_LIMIT_BYTES),
        name="layer_a_shortconv",
    )(x, ln_g.reshape(1, d), w_in, conv_w, w_out, *cast_weights)
    return outs[0], tuple(outs[1:])


def _spread(n_slots, n_items):
    return [((k + 1) * n_slots) // (n_items + 1) for k in range(n_items)]


def _layer_b_kernel(x_ref, lnkv_ref, lnb_ref, wkv_ref, kg_ref, cost_ref, sint_ref,
                    winb_ref, perm_ref, qga_ref, qgb_ref, cq_ref, sq_ref, sink_ref, woutb_ref,
                    o_ref, kt_scr, v_scr, wq_scr):
    tile, d_model = x_ref.shape
    kv_w = N_KV_HEADS * HEAD_DIM
    grp_w = GROUP * HEAD_DIM
    first_tile = pl.program_id(1) == 0

    @pl.when(jnp.logical_and(pl.program_id(0) == 0, first_tile))
    def _():
        for g in range(N_KV_HEADS):
            cols = slice(g * grp_w, (g + 1) * grp_w)
            wq_scr[:, cols] = jnp.dot(winb_ref[:, cols], perm_ref[...], preferred_element_type=F32).astype(BF16)

    @pl.when(first_tile)
    def _():
        kt_scr[:, :, 0:BLOCK] = jnp.zeros((N_KV_HEADS, grp_w, BLOCK), BF16)
        v_scr[:, 0:BLOCK, :] = jnp.zeros((N_KV_HEADS, BLOCK, grp_w), BF16)

    @pl.when(jnp.logical_not(first_tile))
    def _():
        kt_scr[:, :, 0:BLOCK] = kt_scr[:, :, tile:tile + BLOCK]
        v_scr[:, 0:BLOCK, :] = v_scr[:, tile:tile + BLOCK, :]

    lane_c = lax.broadcasted_iota(jnp.int32, (CHUNK, LANES), 1)
    lo_c = lane_c < HEAD_DIM
    quarter_c = [jnp.logical_and(lane_c >= hh * HALF_DIM, lane_c < (hh + 1) * HALF_DIM) for hh in range(GROUP)]
    lane_g = lax.broadcasted_iota(jnp.int32, (BLOCK, grp_w), 1) % LANES
    quarter_g = [jnp.logical_and(lane_g >= hh * HALF_DIM, lane_g < (hh + 1) * HALF_DIM) for hh in range(GROUP)]
    lo_b = lax.broadcasted_iota(jnp.int32, (BLOCK, LANES), 1) < HEAD_DIM
    lo_row = lax.broadcasted_iota(jnp.int32, (1, LANES), 1) < HEAD_DIM
    prev_valid = (lax.broadcasted_iota(jnp.int32, (BLOCK, BLOCK), 1)
                  > lax.broadcasted_iota(jnp.int32, (BLOCK, BLOCK), 0) + (BLOCK - WINDOW))
    ones = jnp.ones((CHUNK, LANES), BF16)

    piece_w = DENSE_PIECE
    n_pieces = d_model // piece_w
    n_chunks = tile // CHUNK
    n_qb = CHUNK // BLOCK
    n_iter = n_qb * N_KV_HEADS
    h_kv = [None] * n_chunks
    h_b = [None] * n_chunks
    q_groups = [[None] * N_KV_HEADS for _ in range(n_chunks)]
    gates = [[None] * n_pieces for _ in range(n_chunks)]
    attn = [[[None] * (2 * N_KV_HEADS) for _ in range(n_qb)] for _ in range(n_chunks)]
    gated = [None] * n_chunks

    def prep(i):
        x = x_ref[i * CHUNK:(i + 1) * CHUNK, :]
        xn = x * _rms_scale(x)
        h_kv[i] = (xn * lnkv_ref[...]).astype(BF16)
        h_b[i] = (xn * lnb_ref[...]).astype(BF16)

    def kv_piece(i):
        k0 = BLOCK + i * CHUNK
        kv = jnp.dot(h_kv[i], wkv_ref[...], preferred_element_type=F32)
        k_t = kv[:, 0:kv_w].T
        cos_t = cost_ref[i]
        sin_t = sint_ref[i]
        for g in range(N_KV_HEADS):
            kh = k_t[g * HEAD_DIM:(g + 1) * HEAD_DIM, :]
            kh = kh * lax.rsqrt(jnp.mean(kh * kh, axis=0, keepdims=True) + EPS) * kg_ref[...]
            r1 = kh[0:ROT_HALF, :]
            r2 = kh[ROT_HALF:ROT_DIM, :]
            set_a = jnp.concatenate([r1 * cos_t - r2 * sin_t, kh[ROT_DIM:ROT_DIM + HALF_DIM - ROT_HALF, :]], axis=0)
            set_b = jnp.concatenate([r2 * cos_t + r1 * sin_t, kh[ROT_DIM + HALF_DIM - ROT_HALF:, :]], axis=0)
            kt_scr[g, :, k0:k0 + CHUNK] = jnp.concatenate([set_a] * GROUP + [set_b] * GROUP, axis=0).astype(BF16)
        for c in range(kv_w // LANES):
            vc = kv[:, kv_w + c * LANES:kv_w + (c + 1) * LANES]
            vr = pltpu.roll(vc, HEAD_DIM, axis=1)
            v_scr[2 * c, k0:k0 + CHUNK, 0:LANES] = jnp.where(lo_c, vc, vr).astype(BF16)
            v_scr[2 * c + 1, k0:k0 + CHUNK, 0:LANES] = jnp.where(lo_c, vr, vc).astype(BF16)
            v_scr[2 * c, k0:k0 + CHUNK, LANES:2 * LANES] = ones
            v_scr[2 * c + 1, k0:k0 + CHUNK, LANES:2 * LANES] = ones

    def q_piece(i, p):
        q = jnp.dot(h_b[i], wq_scr[:, p * piece_w:(p + 1) * piece_w], preferred_element_type=F32)
        cq = cq_ref[i * CHUNK:(i + 1) * CHUNK, :]
        sq = sq_ref[i * CHUNK:(i + 1) * CHUNK, :]
        for gg in range(piece_w // grp_w):
            c0 = q[:, gg * grp_w:gg * grp_w + LANES]
            c1 = q[:, gg * grp_w + LANES:(gg + 1) * grp_w]
            t = c0 * c0 + c1 * c1
            ss = [jnp.sum(jnp.where(quarter_c[hh], t, 0.0), axis=-1, keepdims=True) for hh in range(GROUP)]
            ms = jnp.where(quarter_c[0], ss[0], jnp.where(quarter_c[1], ss[1], jnp.where(quarter_c[2], ss[2], ss[3])))
            rstd = lax.rsqrt(ms * (1.0 / HEAD_DIM) + EPS)
            n0 = c0 * rstd * qga_ref[...]
            n1 = c1 * rstd * qgb_ref[...]
            q_groups[i][p * (piece_w // grp_w) + gg] = jnp.concatenate(
                [n0 * cq - n1 * sq, n1 * cq + n0 * sq], axis=1).astype(BF16)

    def z_piece(i, p):
        z = jnp.dot(h_b[i], winb_ref[:, d_model + p * piece_w:d_model + (p + 1) * piece_w],
                    preferred_element_type=F32)
        gates[i][p] = z * jax.nn.sigmoid(z)

    def o_piece(i, p):
        if gated[i] is None:
            full = jnp.concatenate([jnp.concatenate(attn[i][qb], axis=1) for qb in range(n_qb)], axis=0)
            gated[i] = (full * jnp.concatenate(gates[i], axis=1)).astype(BF16)
        rows = slice(i * CHUNK, (i + 1) * CHUNK)
        cols = slice(p * piece_w, (p + 1) * piece_w)
        o_ref[rows, cols] = x_ref[rows, cols] + jnp.dot(gated[i], woutb_ref[:, cols], preferred_element_type=F32)

    def scores(i, t):
        qb, g = divmod(t, N_KV_HEADS)
        k_lo = i * CHUNK + qb * BLOCK
        qg = q_groups[i][g][qb * BLOCK:(qb + 1) * BLOCK, :]
        q4 = jnp.concatenate([jnp.where(quarter_g[hh], qg, jnp.zeros_like(qg)) for hh in range(GROUP)], axis=0)
        return jnp.dot(q4, kt_scr[g, :, k_lo:k_lo + 2 * BLOCK], preferred_element_type=F32)

    def attend(i, t, s):
        qb, g = divmod(t, N_KV_HEADS)
        k_lo = i * CHUNK + qb * BLOCK
        probs = []
        row_max = []
        for hh in range(GROUP):
            s_prev = s[hh * BLOCK:(hh + 1) * BLOCK, 0:BLOCK]
            s_own = s[hh * BLOCK:(hh + 1) * BLOCK, BLOCK:2 * BLOCK]
            if i == 0 and qb == 0:
                s_prev = jnp.minimum(s_prev, jnp.where(first_tile, NEG_INF, FLT_MAX))
            sm = jnp.where(prev_valid, s_prev, s_own)
            m = jnp.max(sm, axis=-1, keepdims=True)
            p = jnp.exp2(sm - m).astype(BF16)
            zero = jnp.zeros_like(p)
            probs.append(jnp.concatenate([jnp.where(prev_valid, p, zero), jnp.where(prev_valid, zero, p)], axis=1))
            row_max.append(m)
        o4 = jnp.dot(jnp.concatenate(probs, axis=0), v_scr[g, k_lo:k_lo + 2 * BLOCK, :],
                     preferred_element_type=F32)
        for pair in range(GROUP // 2):
            h0, h1 = 2 * pair, 2 * pair + 1
            sink = jnp.where(lo_row, sink_ref[g * GROUP + h0] * LOG2E, sink_ref[g * GROUP + h1] * LOG2E)
            blk0 = o4[h0 * BLOCK:(h0 + 1) * BLOCK, :]
            blk1 = o4[h1 * BLOCK:(h1 + 1) * BLOCK, :]
            den = (jnp.where(lo_b, blk0[:, LANES:2 * LANES], blk1[:, LANES:2 * LANES])
                   + jnp.exp2(sink - jnp.where(lo_b, row_max[h0], row_max[h1])))
            attn[i][qb][2 * g + pair] = jnp.where(lo_b, blk0[:, 0:LANES], blk1[:, 0:LANES]) * (1.0 / den)

    prep(0)
    kv_piece(0)
    for p in range(n_pieces):
        q_piece(0, p)
    for i in range(n_chunks):
        fillers = []
        if i > 0:
            fillers += [functools.partial(o_piece, i - 1, p) for p in range(n_pieces)]
        fillers += [functools.partial(z_piece, i, p) for p in range(n_pieces)]
        if i + 1 < n_chunks:
            prep(i + 1)
            fillers += [functools.partial(kv_piece, i + 1)]
            fillers += [functools.partial(q_piece, i + 1, p) for p in range(n_pieces)]
        after = _spread(n_iter, len(fillers))
        pending = [scores(i, t) for t in range(SCORE_LOOKAHEAD)]
        for t in range(n_iter):
            if t + SCORE_LOOKAHEAD < n_iter:
                pending.append(scores(i, t + SCORE_LOOKAHEAD))
            for k, slot in enumerate(after):
                if slot == t:
                    fillers[k]()
            attend(i, t, pending.pop(0))
    for p in range(n_pieces):
        o_piece(n_chunks - 1, p)


def _rope_tables(seq):
    pos = np.arange(seq, dtype=np.float32)
    inv = np.float32(ROPE_THETA) ** (-np.arange(0, ROT_DIM, 2, dtype=np.float32) / np.float32(ROT_DIM))
    ang = pos[:, None] * inv[None, :]
    return np.cos(ang), np.sin(ang)


def _dim_sets(t):
    mid = ROT_DIM + HALF_DIM - ROT_HALF
    set_a = jnp.concatenate([t[..., 0:ROT_HALF], t[..., ROT_DIM:mid]], axis=-1)
    set_b = jnp.concatenate([t[..., ROT_HALF:ROT_DIM], t[..., mid:HEAD_DIM]], axis=-1)
    return set_a, set_b


def _group_permutation():
    perm = np.zeros((GROUP * HEAD_DIM, GROUP * HEAD_DIM), np.float32)
    for si, dims in enumerate((DIMS_A, DIMS_B)):
        for hh in range(GROUP):
            new = si * GROUP * HALF_DIM + hh * HALF_DIM + np.arange(HALF_DIM)
            perm[hh * HEAD_DIM + dims, new] = 1.0
    return perm.astype(BF16)


def _layer_b(x, ln_kv, w_kv, k_norm, ln_b, w_in_b, q_norm, sinks, w_out_b):
    bsz, seq, d = x.shape
    kv2 = w_kv.shape[1]
    grp_w = GROUP * HEAD_DIM
    scale = LOG2E / math.sqrt(HEAD_DIM)

    cos, sin = _rope_tables(seq)
    cos_t = np.ascontiguousarray(cos.T.reshape(ROT_HALF, seq // CHUNK, CHUNK).transpose(1, 0, 2))
    sin_t = np.ascontiguousarray(sin.T.reshape(ROT_HALF, seq // CHUNK, CHUNK).transpose(1, 0, 2))
    jj = np.arange(LANES) % HALF_DIM
    rot = jj < ROT_HALF
    cq = (np.where(rot, cos[:, jj % ROT_HALF], 1.0) * scale).astype(np.float32)
    sq = (np.where(rot, sin[:, jj % ROT_HALF], 0.0) * scale).astype(np.float32)
    gain_a, gain_b = _dim_sets(q_norm)
    qg_a = jnp.tile(gain_a, GROUP).reshape(1, LANES)
    qg_b = jnp.tile(gain_b, GROUP).reshape(1, LANES)
    perm = _group_permutation()

    const2 = lambda b, j: (0, 0)
    single = pl.Buffered(1)
    return pl.pallas_call(
        _layer_b_kernel,
        out_shape=jax.ShapeDtypeStruct(x.shape, x.dtype),
        grid=(bsz, seq // SEQ_TILE),
        in_specs=[
            pl.BlockSpec((None, SEQ_TILE, d), lambda b, j: (b, j, 0)),
            pl.BlockSpec((1, d), const2),
            pl.BlockSpec((1, d), const2),
            pl.BlockSpec((d, kv2), const2, pipeline_mode=single),
            pl.BlockSpec((HEAD_DIM, 1), const2),
            pl.BlockSpec((SEQ_TILE // CHUNK, ROT_HALF, CHUNK), lambda b, j: (j, 0, 0)),
            pl.BlockSpec((SEQ_TILE // CHUNK, ROT_HALF, CHUNK), lambda b, j: (j, 0, 0)),
            pl.BlockSpec((d, 2 * d), const2, pipeline_mode=single),
            pl.BlockSpec((grp_w, grp_w), const2),
            pl.BlockSpec((1, LANES), const2),
            pl.BlockSpec((1, LANES), const2),
            pl.BlockSpec((SEQ_TILE, LANES), lambda b, j: (j, 0)),
            pl.BlockSpec((SEQ_TILE, LANES), lambda b, j: (j, 0)),
            pl.BlockSpec(memory_space=pltpu.SMEM),
            pl.BlockSpec((d, d), const2, pipeline_mode=single),
        ],
        out_specs=pl.BlockSpec((None, SEQ_TILE, d), lambda b, j: (b, j, 0)),
        scratch_shapes=[
            pltpu.VMEM((N_KV_HEADS, grp_w, BLOCK + SEQ_TILE), BF16),
            pltpu.VMEM((N_KV_HEADS, BLOCK + SEQ_TILE, grp_w), BF16),
            pltpu.VMEM((d, d), BF16),
        ],
        compiler_params=pltpu.CompilerParams(
            dimension_semantics=("arbitrary", "arbitrary"),
            vmem_limit_bytes=VMEM_LIMIT_BYTES),
        name="layer_b_swa",
    )(x, ln_kv.reshape(1, d), ln_b.reshape(1, d), w_kv.astype(BF16), k_norm.reshape(HEAD_DIM, 1),
      cos_t, sin_t, w_in_b.astype(BF16), perm, qg_a, qg_b, cq, sq, sinks.astype(F32), w_out_b.astype(BF16))


def kernel(x, ln_a, w_in_a, conv_a, w_out_a, ln_kv, w_kv, k_norm, ln_b, w_in_b, q_norm, sinks, w_out_b):
    n_a = ln_a.shape[0]
    n_b = ln_b.shape[0]
    assert n_b == 1, "the shared K/V is recomputed inside the single attention layer"
    assert n_a >= 1
    for i in range(n_a):
        nxt = (w_kv, w_in_b[0], w_out_b[0]) if i == n_a - 1 else ()
        x, cast = _layer_a(x, ln_a[i], w_in_a[i], conv_a[i], w_out_a[i], nxt)
    w_kv16, w_in_b16, w_out_b16 = cast
    return _layer_b(x, ln_kv, w_kv16, k_norm, ln_b[0], w_in_b16, q_norm[0], sinks[0], w_out_b16)
```

```python
import functools
import math

import numpy as np

import jax
import jax.numpy as jnp
from jax import lax
from jax.experimental import pallas as pl
from jax.experimental.pallas import tpu as pltpu

F32 = jnp.float32
BF16 = jnp.bfloat16

HEAD_DIM = 64
N_KV_HEADS = 4
GROUP = 4
N_HEADS = N_KV_HEADS * GROUP
WINDOW = 128
BLOCK = 128
assert WINDOW == BLOCK
ROT_DIM = HEAD_DIM // 4
ROT_HALF = ROT_DIM // 2
ROPE_THETA = 500000.0
EPS = 1e-6
NEG_INF = -1e30
FLT_MAX = float(np.finfo(np.float32).max)
LOG2E = math.log2(math.e)
CONV_W = 3

LANES = 128
SUBLANES = 8
VMEM_LIMIT_BYTES = 56 * 1024 * 1024

SEQ_TILE = 1024
CHUNK = 512
SCORE_LOOKAHEAD = 2
DENSE_PIECE = 512

HALF_DIM = HEAD_DIM // 2
DIMS_A = np.concatenate([np.arange(0, ROT_HALF), np.arange(ROT_DIM, ROT_DIM + HALF_DIM - ROT_HALF)])
DIMS_B = np.concatenate([np.arange(ROT_HALF, ROT_DIM), np.arange(ROT_DIM + HALF_DIM - ROT_HALF, HEAD_DIM)])


def _rms_scale(x):
    return lax.rsqrt(jnp.mean(x * x, axis=-1, keepdims=True) + EPS)


def _layer_a_kernel(*refs, width, n_cast):
    x_ref, g_ref, win_ref, cw_ref, wout_ref = refs[:5]
    cast_in = refs[5:5 + n_cast]
    o_ref = refs[5 + n_cast]
    cast_out = refs[6 + n_cast:6 + 2 * n_cast]
    vbuf = refs[6 + 2 * n_cast]
    w = width
    tile = x_ref.shape[0]
    n_chunks = tile // CHUNK

    @pl.when(pl.program_id(1) == 0)
    def _():
        vbuf[0:SUBLANES, :] = jnp.zeros((SUBLANES, w), F32)

    @pl.when(pl.program_id(1) != 0)
    def _():
        vbuf[0:SUBLANES, :] = vbuf[tile:tile + SUBLANES, :]

    ys = []
    for i in range(n_chunks):
        r0 = i * CHUNK
        v0 = SUBLANES + r0
        x = x_ref[r0:r0 + CHUNK, :]
        h = (x * _rms_scale(x) * g_ref[...]).astype(BF16)
        cu = jnp.dot(h, win_ref[:, w:3 * w].astype(BF16), preferred_element_type=F32)
        vbuf[v0:v0 + CHUNK, :] = cu[:, 0:w] * cu[:, w:2 * w]
        conv = (cw_ref[2] * vbuf[v0:v0 + CHUNK, :]
                + cw_ref[1] * vbuf[v0 - 1:v0 - 1 + CHUNK, :]
                + cw_ref[0] * vbuf[v0 - 2:v0 - 2 + CHUNK, :])
        z = jnp.dot(h, win_ref[:, 3 * w:4 * w].astype(BF16), preferred_element_type=F32)
        gate = conv * (z * jax.nn.sigmoid(z))
        b_gate = jnp.dot(h, win_ref[:, 0:w].astype(BF16), preferred_element_type=F32)
        ys.append((b_gate * gate).astype(BF16))
    for i in range(n_chunks):
        r0 = i * CHUNK
        o_ref[r0:r0 + CHUNK, :] = x_ref[r0:r0 + CHUNK, :] + jnp.dot(
            ys[i], wout_ref[...].astype(BF16), preferred_element_type=F32)
    for src, dst in zip(cast_in, cast_out):
        dst[...] = src[...].astype(BF16)


def _layer_a(x, ln_g, w_in, conv_w, w_out, cast_weights=()):
    bsz, seq, d = x.shape
    w = w_out.shape[0]
    n_j = seq // SEQ_TILE
    n_steps = bsz * n_j
    const = lambda b, j: (0, 0)
    slab = lambda b, j: (b * n_j + j, 0)
    single = pl.Buffered(1)
    cast_specs = [pl.BlockSpec((cw.shape[0] // n_steps, cw.shape[1]), slab) for cw in cast_weights]
    outs = pl.pallas_call(
        functools.partial(_layer_a_kernel, width=w, n_cast=len(cast_weights)),
        out_shape=[jax.ShapeDtypeStruct(x.shape, x.dtype)]
        + [jax.ShapeDtypeStruct(cw.shape, BF16) for cw in cast_weights],
        grid=(bsz, n_j),
        in_specs=[
            pl.BlockSpec((None, SEQ_TILE, d), lambda b, j: (b, j, 0)),
            pl.BlockSpec((1, d), const),
            pl.BlockSpec((d, 4 * w), const, pipeline_mode=single),
            pl.BlockSpec((CONV_W, 1, w), lambda b, j: (0, 0, 0)),
            pl.BlockSpec((w, d), const, pipeline_mode=single),
        ] + cast_specs,
        out_specs=[pl.BlockSpec((None, SEQ_TILE, d), lambda b, j: (b, j, 0))] + cast_specs,
        scratch_shapes=[pltpu.VMEM((SUBLANES + SEQ_TILE, w), F32)],
        compiler_params=pltpu.CompilerParams(
            dimension_semantics=("arbitrary", "arbitrary"),
            vmem_limit_bytes=VMEM_LIMIT_BYTES),
        name="layer_a_shortconv",
    )(x, ln_g.reshape(1, d), w_in, conv_w, w_out, *cast_weights)
    return outs[0], tuple(outs[1:])


def _spread(n_slots, n_items):
    return [((k + 1) * n_slots) // (n_items + 1) for k in range(n_items)]


def _layer_b_kernel(x_ref, lnkv_ref, lnb_ref, wkv_ref, kg_ref, cost_ref, sint_ref,
                    winb_ref, perm_ref, qga_ref, qgb_ref, cq_ref, sq_ref, sink_ref, woutb_ref,
                    o_ref, kt_scr, v_scr, wq_scr):
    tile, d_model = x_ref.shape
    kv_w = N_KV_HEADS * HEAD_DIM
    grp_w = GROUP * HEAD_DIM
    first_tile = pl.program_id(1) == 0

    @pl.when(jnp.logical_and(pl.program_id(0) == 0, first_tile))
    def _():
        for g in range(N_KV_HEADS):
            cols = slice(g * grp_w, (g + 1) * grp_w)
            wq_scr[:, cols] = jnp.dot(winb_ref[:, cols], perm_ref[...], preferred_element_type=F32).astype(BF16)

    @pl.when(first_tile)
    def _():
        kt_scr[:, :, 0:BLOCK] = jnp.zeros((N_KV_HEADS, grp_w, BLOCK), BF16)
        v_scr[:, 0:BLOCK, :] = jnp.zeros((N_KV_HEADS, BLOCK, grp_w), BF16)

    @pl.when(jnp.logical_not(first_tile))
    def _():
        kt_scr[:, :, 0:BLOCK] = kt_scr[:, :, tile:tile + BLOCK]
        v_scr[:, 0:BLOCK, :] = v_scr[:, tile:tile + BLOCK, :]

    lane_c = lax.broadcasted_iota(jnp.int32, (CHUNK, LANES), 1)
    lo_c = lane_c < HEAD_DIM
    quarter_c = [jnp.logical_and(lane_c >= hh * HALF_DIM, lane_c < (hh + 1) * HALF_DIM) for hh in range(GROUP)]
    lane_g = lax.broadcasted_iota(jnp.int32, (BLOCK, grp_w), 1) % LANES
    quarter_g = [jnp.logical_and(lane_g >= hh * HALF_DIM, lane_g < (hh + 1) * HALF_DIM) for hh in range(GROUP)]
    lo_b = lax.broadcasted_iota(jnp.int32, (BLOCK, LANES), 1) < HEAD_DIM
    lo_row = lax.broadcasted_iota(jnp.int32, (1, LANES), 1) < HEAD_DIM
    prev_valid = (lax.broadcasted_iota(jnp.int32, (BLOCK, BLOCK), 1)
                  > lax.broadcasted_iota(jnp.int32, (BLOCK, BLOCK), 0) + (BLOCK - WINDOW))
    ones = jnp.ones((CHUNK, LANES), BF16)

    piece_w = DENSE_PIECE
    n_pieces = d_model // piece_w
    n_chunks = tile // CHUNK
    n_qb = CHUNK // BLOCK
    n_iter = n_qb * N_KV_HEADS
    h_kv = [None] * n_chunks
    h_b = [None] * n_chunks
    q_groups = [[None] * N_KV_HEADS for _ in range(n_chunks)]
    gates = [[None] * n_pieces for _ in range(n_chunks)]
    attn = [[[None] * (2 * N_KV_HEADS) for _ in range(n_qb)] for _ in range(n_chunks)]
    gated = [None] * n_chunks

    def prep(i):
        x = x_ref[i * CHUNK:(i + 1) * CHUNK, :]
        xn = x * _rms_scale(x)
        h_kv[i] = (xn * lnkv_ref[...]).astype(BF16)
        h_b[i] = (xn * lnb_ref[...]).astype(BF16)

    def kv_piece(i):
        k0 = BLOCK + i * CHUNK
        kv = jnp.dot(h_kv[i], wkv_ref[...], preferred_element_type=F32)
        k_t = kv[:, 0:kv_w].T
        cos_t = cost_ref[i]
        sin_t = sint_ref[i]
        for g in range(N_KV_HEADS):
            kh = k_t[g * HEAD_DIM:(g + 1) * HEAD_DIM, :]
            kh = kh * lax.rsqrt(jnp.mean(kh * kh, axis=0, keepdims=True) + EPS) * kg_ref[...]
            r1 = kh[0:ROT_HALF, :]
            r2 = kh[ROT_HALF:ROT_DIM, :]
            set_a = jnp.concatenate([r1 * cos_t - r2 * sin_t, kh[ROT_DIM:ROT_DIM + HALF_DIM - ROT_HALF, :]], axis=0)
            set_b = jnp.concatenate([r2 * cos_t + r1 * sin_t, kh[ROT_DIM + HALF_DIM - ROT_HALF:, :]], axis=0)
            kt_scr[g, :, k0:k0 + CHUNK] = jnp.concatenate([set_a] * GROUP + [set_b] * GROUP, axis=0).astype(BF16)
        for c in range(kv_w // LANES):
            vc = kv[:, kv_w + c * LANES:kv_w + (c + 1) * LANES]
            vr = pltpu.roll(vc, HEAD_DIM, axis=1)
            v_scr[2 * c, k0:k0 + CHUNK, 0:LANES] = jnp.where(lo_c, vc, vr).astype(BF16)
            v_scr[2 * c + 1, k0:k0 + CHUNK, 0:LANES] = jnp.where(lo_c, vr, vc).astype(BF16)
            v_scr[2 * c, k0:k0 + CHUNK, LANES:2 * LANES] = ones
            v_scr[2 * c + 1, k0:k0 + CHUNK, LANES:2 * LANES] = ones

    def q_piece(i, p):
        q = jnp.dot(h_b[i], wq_scr[:, p * piece_w:(p + 1) * piece_w], preferred_element_type=F32)
        cq = cq_ref[i * CHUNK:(i + 1) * CHUNK, :]
        sq = sq_ref[i * CHUNK:(i + 1) * CHUNK, :]
        for gg in range(piece_w // grp_w):
            c0 = q[:, gg * grp_w:gg * grp_w + LANES]
            c1 = q[:, gg * grp_w + LANES:(gg + 1) * grp_w]
            t = c0 * c0 + c1 * c1
            ss = [jnp.sum(jnp.where(quarter_c[hh], t, 0.0), axis=-1, keepdims=True) for hh in range(GROUP)]
            ms = jnp.where(quarter_c[0], ss[0], jnp.where(quarter_c[1], ss[1], jnp.where(quarter_c[2], ss[2], ss[3])))
            rstd = lax.rsqrt(ms * (1.0 / HEAD_DIM) + EPS)
            n0 = c0 * rstd * qga_ref[...]
            n1 = c1 * rstd * qgb_ref[...]
            q_groups[i][p * (piece_w // grp_w) + gg] = jnp.concatenate(
                [n0 * cq - n1 * sq, n1 * cq + n0 * sq], axis=1).astype(BF16)

    def z_piece(i, p):
        z = jnp.dot(h_b[i], winb_ref[:, d_model + p * piece_w:d_model + (p + 1) * piece_w],
                    preferred_element_type=F32)
        gates[i][p] = z * jax.nn.sigmoid(z)

    def o_piece(i, p):
        if gated[i] is None:
            full = jnp.concatenate([jnp.concatenate(attn[i][qb], axis=1) for qb in range(n_qb)], axis=0)
            gated[i] = (full * jnp.concatenate(gates[i], axis=1)).astype(BF16)
        rows = slice(i * CHUNK, (i + 1) * CHUNK)
        cols = slice(p * piece_w, (p + 1) * piece_w)
        o_ref[rows, cols] = x_ref[rows, cols] + jnp.dot(gated[i], woutb_ref[:, cols], preferred_element_type=F32)

    def scores(i, t):
        qb, g = divmod(t, N_KV_HEADS)
        k_lo = i * CHUNK + qb * BLOCK
        qg = q_groups[i][g][qb * BLOCK:(qb + 1) * BLOCK, :]
        q4 = jnp.concatenate([jnp.where(quarter_g[hh], qg, jnp.zeros_like(qg)) for hh in range(GROUP)], axis=0)
        return jnp.dot(q4, kt_scr[g, :, k_lo:k_lo + 2 * BLOCK], preferred_element_type=F32)

    def attend(i, t, s):
        qb, g = divmod(t, N_KV_HEADS)
        k_lo = i * CHUNK + qb * BLOCK
        probs = []
        row_max = []
        for hh in range(GROUP):
            s_prev = s[hh * BLOCK:(hh + 1) * BLOCK, 0:BLOCK]
            s_own = s[hh * BLOCK:(hh + 1) * BLOCK, BLOCK:2 * BLOCK]
            if i == 0 and qb == 0:
                s_prev = jnp.minimum(s_prev, jnp.where(first_tile, NEG_INF, FLT_MAX))
            sm = jnp.where(prev_valid, s_prev, s_own)
            m = jnp.max(sm, axis=-1, keepdims=True)
            p = jnp.exp2((sm - m).astype(BF16))
            zero = jnp.zeros_like(p)
            probs.append(jnp.concatenate([jnp.where(prev_valid, p, zero), jnp.where(prev_valid, zero, p)], axis=1))
            row_max.append(m)
        o4 = jnp.dot(jnp.concatenate(probs, axis=0), v_scr[g, k_lo:k_lo + 2 * BLOCK, :],
                     preferred_element_type=F32)
        for pair in range(GROUP // 2):
            h0, h1 = 2 * pair, 2 * pair + 1
            sink = jnp.where(lo_row, sink_ref[g * GROUP + h0] * LOG2E, sink_ref[g * GROUP + h1] * LOG2E)
            blk0 = o4[h0 * BLOCK:(h0 + 1) * BLOCK, :]
            blk1 = o4[h1 * BLOCK:(h1 + 1) * BLOCK, :]
            den = (jnp.where(lo_b, blk0[:, LANES:2 * LANES], blk1[:, LANES:2 * LANES])
                   + jnp.exp2(sink - jnp.where(lo_b, row_max[h0], row_max[h1])))
            attn[i][qb][2 * g + pair] = jnp.where(lo_b, blk0[:, 0:LANES], blk1[:, 0:LANES]) * (1.0 / den)

    prep(0)
    kv_piece(0)
    for p in range(n_pieces):
        q_piece(0, p)
    for i in range(n_chunks):
        fillers = []
        if i > 0:
            fillers += [functools.partial(o_piece, i - 1, p) for p in range(n_pieces)]
        fillers += [functools.partial(z_piece, i, p) for p in range(n_pieces)]
        if i + 1 < n_chunks:
            prep(i + 1)
            fillers += [functools.partial(kv_piece, i + 1)]
            fillers += [functools.partial(q_piece, i + 1, p) for p in range(n_pieces)]
        after = _spread(n_iter, len(fillers))
        pending = [scores(i, t) for t in range(SCORE_LOOKAHEAD)]
        for t in range(n_iter):
            if t + SCORE_LOOKAHEAD < n_iter:
                pending.append(scores(i, t + SCORE_LOOKAHEAD))
            for k, slot in enumerate(after):
                if slot == t:
                    fillers[k]()
            attend(i, t, pending.pop(0))
    for p in range(n_pieces):
        o_piece(n_chunks - 1, p)


def _rope_tables(seq):
    pos = np.arange(seq, dtype=np.float32)
    inv = np.float32(ROPE_THETA) ** (-np.arange(0, ROT_DIM, 2, dtype=np.float32) / np.float32(ROT_DIM))
    ang = pos[:, None] * inv[None, :]
    return np.cos(ang), np.sin(ang)


def _dim_sets(t):
    mid = ROT_DIM + HALF_DIM - ROT_HALF
    set_a = jnp.concatenate([t[..., 0:ROT_HALF], t[..., ROT_DIM:mid]], axis=-1)
    set_b = jnp.concatenate([t[..., ROT_HALF:ROT_DIM], t[..., mid:HEAD_DIM]], axis=-1)
    return set_a, set_b


def _group_permutation():
    perm = np.zeros((GROUP * HEAD_DIM, GROUP * HEAD_DIM), np.float32)
    for si, dims in enumerate((DIMS_A, DIMS_B)):
        for hh in range(GROUP):
            new = si * GROUP * HALF_DIM + hh * HALF_DIM + np.arange(HALF_DIM)
            perm[hh * HEAD_DIM + dims, new] = 1.0
    return perm.astype(BF16)


def _layer_b(x, ln_kv, w_kv, k_norm, ln_b, w_in_b, q_norm, sinks, w_out_b):
    bsz, seq, d = x.shape
    kv2 = w_kv.shape[1]
    grp_w = GROUP * HEAD_DIM
    scale = LOG2E / math.sqrt(HEAD_DIM)

    cos, sin = _rope_tables(seq)
    cos_t = np.ascontiguousarray(cos.T.reshape(ROT_HALF, seq // CHUNK, CHUNK).transpose(1, 0, 2))
    sin_t = np.ascontiguousarray(sin.T.reshape(ROT_HALF, seq // CHUNK, CHUNK).transpose(1, 0, 2))
    jj = np.arange(LANES) % HALF_DIM
    rot = jj < ROT_HALF
    cq = (np.where(rot, cos[:, jj % ROT_HALF], 1.0) * scale).astype(np.float32)
    sq = (np.where(rot, sin[:, jj % ROT_HALF], 0.0) * scale).astype(np.float32)
    gain_a, gain_b = _dim_sets(q_norm)
    qg_a = jnp.tile(gain_a, GROUP).reshape(1, LANES)
    qg_b = jnp.tile(gain_b, GROUP).reshape(1, LANES)
    perm = _group_permutation()

    const2 = lambda b, j: (0, 0)
    single = pl.Buffered(1)
    return pl.pallas_call(
        _layer_b_kernel,
        out_shape=jax.ShapeDtypeStruct(x.shape, x.dtype),
        grid=(bsz, seq // SEQ_TILE),
        in_specs=[
            pl.BlockSpec((None, SEQ_TILE, d), lambda b, j: (b, j, 0)),
            pl.BlockSpec((1, d), const2),
            pl.BlockSpec((1, d), const2),
            pl.BlockSpec((d, kv2), const2, pipeline_mode=single),
            pl.BlockSpec((HEAD_DIM, 1), const2),
            pl.BlockSpec((SEQ_TILE // CHUNK, ROT_HALF, CHUNK), lambda b, j: (j, 0, 0)),
            pl.BlockSpec((SEQ_TILE // CHUNK, ROT_HALF, CHUNK), lambda b, j: (j, 0, 0)),
            pl.BlockSpec((d, 2 * d), const2, pipeline_mode=single),
            pl.BlockSpec((grp_w, grp_w), const2),
            pl.BlockSpec((1, LANES), const2),
            pl.BlockSpec((1, LANES), const2),
            pl.BlockSpec((SEQ_TILE, LANES), lambda b, j: (j, 0)),
            pl.BlockSpec((SEQ_TILE, LANES), lambda b, j: (j, 0)),
            pl.BlockSpec(memory_space=pltpu.SMEM),
            pl.BlockSpec((d, d), const2, pipeline_mode=single),
        ],
        out_specs=pl.BlockSpec((None, SEQ_TILE, d), lambda b, j: (b, j, 0)),
        scratch_shapes=[
            pltpu.VMEM((N_KV_HEADS, grp_w, BLOCK + SEQ_TILE), BF16),
            pltpu.VMEM((N_KV_HEADS, BLOCK + SEQ_TILE, grp_w), BF16),
            pltpu.VMEM((d, d), BF16),
        ],
        compiler_params=pltpu.CompilerParams(
            dimension_semantics=("arbitrary", "arbitrary"),
            vmem_limit_bytes=VMEM_LIMIT_BYTES),
        name="layer_b_swa",
    )(x, ln_kv.reshape(1, d), ln_b.reshape(1, d), w_kv.astype(BF16), k_norm.reshape(HEAD_DIM, 1),
      cos_t, sin_t, w_in_b.astype(BF16), perm, qg_a, qg_b, cq, sq, sinks.astype(F32), w_out_b.astype(BF16))


def kernel(x, ln_a, w_in_a, conv_a, w_out_a, ln_kv, w_kv, k_norm, ln_b, w_in_b, q_norm, sinks, w_out_b):
    n_a = ln_a.shape[0]
    n_b = ln_b.shape[0]
    assert n_b == 1, "the shared K/V is recomputed inside the single attention layer"
    assert n_a >= 1
    for i in range(n_a):
        nxt = (w_kv, w_in_b[0], w_out_b[0]) if i == n_a - 1 else ()
        x, cast = _layer_a(x, ln_a[i], w_in_a[i], conv_a[i], w_out_a[i], nxt)
    w_kv16, w_in_b16, w_out_b16 = cast
    return _layer_b(x, ln_kv, w_kv16, k_norm, ln_b[0], w_in_b16, q_norm[0], sinks[0], w_out_b16)
```

```python
import functools
import math

import numpy as np

import jax
import jax.numpy as jnp
from jax import lax
from jax.experimental import pallas as pl
from jax.experimental.pallas import tpu as pltpu

F32 = jnp.float32
BF16 = jnp.bfloat16

HEAD_DIM = 64
N_KV_HEADS = 4
GROUP = 4
N_HEADS = N_KV_HEADS * GROUP
WINDOW = 128
BLOCK = 128
assert WINDOW == BLOCK
ROT_DIM = HEAD_DIM // 4
ROT_HALF = ROT_DIM // 2
ROPE_THETA = 500000.0
EPS = 1e-6
NEG_INF = -1e30
FLT_MAX = float(np.finfo(np.float32).max)
LOG2E = math.log2(math.e)
CONV_W = 3

LANES = 128
SUBLANES = 8
VMEM_LIMIT_BYTES = 56 * 1024 * 1024

SEQ_TILE = 1024
CHUNK = 512
SCORE_LOOKAHEAD = 2
DENSE_PIECE = 512

HALF_DIM = HEAD_DIM // 2
DIMS_A = np.concatenate([np.arange(0, ROT_HALF), np.arange(ROT_DIM, ROT_DIM + HALF_DIM - ROT_HALF)])
DIMS_B = np.concatenate([np.arange(ROT_HALF, ROT_DIM), np.arange(ROT_DIM + HALF_DIM - ROT_HALF, HEAD_DIM)])


def _rms_scale(x):
    return lax.rsqrt(jnp.mean(x * x, axis=-1, keepdims=True) + EPS)


def _layer_a_kernel(*refs, width, n_cast):
    x_ref, g_ref, win_ref, cw_ref, wout_ref = refs[:5]
    cast_in = refs[5:5 + n_cast]
    o_ref = refs[5 + n_cast]
    cast_out = refs[6 + n_cast:6 + 2 * n_cast]
    vbuf = refs[6 + 2 * n_cast]
    w = width
    tile = x_ref.shape[0]
    n_chunks = tile // CHUNK

    @pl.when(pl.program_id(1) == 0)
    def _():
        vbuf[0:SUBLANES, :] = jnp.zeros((SUBLANES, w), F32)

    @pl.when(pl.program_id(1) != 0)
    def _():
        vbuf[0:SUBLANES, :] = vbuf[tile:tile + SUBLANES, :]

    chunks = range(n_chunks)
    hs = []
    for i in chunks:
        x = x_ref[i * CHUNK:(i + 1) * CHUNK, :]
        hs.append((x * _rms_scale(x) * g_ref[...]).astype(BF16))
    w_cu = win_ref[:, w:3 * w].astype(BF16)
    convs = []
    for i in chunks:
        v0 = SUBLANES + i * CHUNK
        cu = jnp.dot(hs[i], w_cu, preferred_element_type=F32)
        vbuf[v0:v0 + CHUNK, :] = cu[:, 0:w] * cu[:, w:2 * w]
        convs.append(cw_ref[2] * vbuf[v0:v0 + CHUNK, :]
                     + cw_ref[1] * vbuf[v0 - 1:v0 - 1 + CHUNK, :]
                     + cw_ref[0] * vbuf[v0 - 2:v0 - 2 + CHUNK, :])
    w_z = win_ref[:, 3 * w:4 * w].astype(BF16)
    gates = []
    for i in chunks:
        z = jnp.dot(hs[i], w_z, preferred_element_type=F32)
        gates.append(convs[i] * (z * jax.nn.sigmoid(z)))
    w_b = win_ref[:, 0:w].astype(BF16)
    ys = [(jnp.dot(hs[i], w_b, preferred_element_type=F32) * gates[i]).astype(BF16) for i in chunks]
    w_o = wout_ref[...].astype(BF16)
    for i in chunks:
        r0 = i * CHUNK
        o_ref[r0:r0 + CHUNK, :] = x_ref[r0:r0 + CHUNK, :] + jnp.dot(ys[i], w_o, preferred_element_type=F32)
    for src, dst in zip(cast_in, cast_out):
        dst[...] = src[...].astype(BF16)


def _layer_a(x, ln_g, w_in, conv_w, w_out, cast_weights=()):
    bsz, seq, d = x.shape
    w = w_out.shape[0]
    n_j = seq // SEQ_TILE
    n_steps = bsz * n_j
    const = lambda b, j: (0, 0)
    slab = lambda b, j: (b * n_j + j, 0)
    single = pl.Buffered(1)
    cast_specs = [pl.BlockSpec((cw.shape[0] // n_steps, cw.shape[1]), slab) for cw in cast_weights]
    outs = pl.pallas_call(
        functools.partial(_layer_a_kernel, width=w, n_cast=len(cast_weights)),
        out_shape=[jax.ShapeDtypeStruct(x.shape, x.dtype)]
        + [jax.ShapeDtypeStruct(cw.shape, BF16) for cw in cast_weights],
        grid=(bsz, n_j),
        in_specs=[
            pl.BlockSpec((None, SEQ_TILE, d), lambda b, j: (b, j, 0)),
            pl.BlockSpec((1, d), const),
            pl.BlockSpec((d, 4 * w), const, pipeline_mode=single),
            pl.BlockSpec((CONV_W, 1, w), lambda b, j: (0, 0, 0)),
            pl.BlockSpec((w, d), const, pipeline_mode=single),
        ] + cast_specs,
        out_specs=[pl.BlockSpec((None, SEQ_TILE, d), lambda b, j: (b, j, 0))] + cast_specs,
        scratch_shapes=[pltpu.VMEM((SUBLANES + SEQ_TILE, w), F32)],
        compiler_params=pltpu.CompilerParams(
            dimension_semantics=("arbitrary", "arbitrary"),
            vmem_limit_bytes=VMEM_LIMIT_BYTES),
        name="layer_a_shortconv",
    )(x, ln_g.reshape(1, d), w_in, conv_w, w_out, *cast_weights)
    return outs[0], tuple(outs[1:])


def _spread(n_slots, n_items):
    return [((k + 1) * n_slots) // (n_items + 1) for k in range(n_items)]


def _layer_b_kernel(x_ref, lnkv_ref, lnb_ref, wkv_ref, kg_ref, cost_ref, sint_ref,
                    winb_ref, perm_ref, qga_ref, qgb_ref, cq_ref, sq_ref, sink_ref, woutb_ref,
                    o_ref, kt_scr, v_scr, wq_scr):
    tile, d_model = x_ref.shape
    kv_w = N_KV_HEADS * HEAD_DIM
    grp_w = GROUP * HEAD_DIM
    first_tile = pl.program_id(1) == 0

    @pl.when(jnp.logical_and(pl.program_id(0) == 0, first_tile))
    def _():
        for g in range(N_KV_HEADS):
            cols = slice(g * grp_w, (g + 1) * grp_w)
            wq_scr[:, cols] = jnp.dot(winb_ref[:, cols], perm_ref[...], preferred_element_type=F32).astype(BF16)

    @pl.when(first_tile)
    def _():
        kt_scr[:, :, 0:BLOCK] = jnp.zeros((N_KV_HEADS, grp_w, BLOCK), BF16)
        v_scr[:, 0:BLOCK, :] = jnp.zeros((N_KV_HEADS, BLOCK, grp_w), BF16)

    @pl.when(jnp.logical_not(first_tile))
    def _():
        kt_scr[:, :, 0:BLOCK] = kt_scr[:, :, tile:tile + BLOCK]
        v_scr[:, 0:BLOCK, :] = v_scr[:, tile:tile + BLOCK, :]

    lane_c = lax.broadcasted_iota(jnp.int32, (CHUNK, LANES), 1)
    lo_c = lane_c < HEAD_DIM
    quarter_c = [jnp.logical_and(lane_c >= hh * HALF_DIM, lane_c < (hh + 1) * HALF_DIM) for hh in range(GROUP)]
    lane_g = lax.broadcasted_iota(jnp.int32, (BLOCK, grp_w), 1) % LANES
    quarter_g = [jnp.logical_and(lane_g >= hh * HALF_DIM, lane_g < (hh + 1) * HALF_DIM) for hh in range(GROUP)]
    lo_b = lax.broadcasted_iota(jnp.int32, (BLOCK, LANES), 1) < HEAD_DIM
    lo_row = lax.broadcasted_iota(jnp.int32, (1, LANES), 1) < HEAD_DIM
    prev_valid = (lax.broadcasted_iota(jnp.int32, (BLOCK, BLOCK), 1)
                  > lax.broadcasted_iota(jnp.int32, (BLOCK, BLOCK), 0) + (BLOCK - WINDOW))
    ones = jnp.ones((CHUNK, LANES), BF16)

    piece_w = DENSE_PIECE
    n_pieces = d_model // piece_w
    n_chunks = tile // CHUNK
    n_qb = CHUNK // BLOCK
    n_iter = n_qb * N_KV_HEADS
    h_kv = [None] * n_chunks
    h_b = [None] * n_chunks
    q_groups = [[None] * N_KV_HEADS for _ in range(n_chunks)]
    gates = [[None] * n_pieces for _ in range(n_chunks)]
    attn = [[[None] * (2 * N_KV_HEADS) for _ in range(n_qb)] for _ in range(n_chunks)]
    gated = [None] * n_chunks

    def prep(i):
        x = x_ref[i * CHUNK:(i + 1) * CHUNK, :]
        xn = x * _rms_scale(x)
        h_kv[i] = (xn * lnkv_ref[...]).astype(BF16)
        h_b[i] = (xn * lnb_ref[...]).astype(BF16)

    def kv_piece(i):
        k0 = BLOCK + i * CHUNK
        kv = jnp.dot(h_kv[i], wkv_ref[...], preferred_element_type=F32)
        k_t = kv[:, 0:kv_w].T
        cos_t = cost_ref[i]
        sin_t = sint_ref[i]
        for g in range(N_KV_HEADS):
            kh = k_t[g * HEAD_DIM:(g + 1) * HEAD_DIM, :]
            kh = kh * lax.rsqrt(jnp.mean(kh * kh, axis=0, keepdims=True) + EPS) * kg_ref[...]
            r1 = kh[0:ROT_HALF, :]
            r2 = kh[ROT_HALF:ROT_DIM, :]
            set_a = jnp.concatenate([r1 * cos_t - r2 * sin_t, kh[ROT_DIM:ROT_DIM + HALF_DIM - ROT_HALF, :]], axis=0)
            set_b = jnp.concatenate([r2 * cos_t + r1 * sin_t, kh[ROT_DIM + HALF_DIM - ROT_HALF:, :]], axis=0)
            kt_scr[g, :, k0:k0 + CHUNK] = jnp.concatenate([set_a] * GROUP + [set_b] * GROUP, axis=0).astype(BF16)
        for c in range(kv_w // LANES):
            vc = kv[:, kv_w + c * LANES:kv_w + (c + 1) * LANES]
            vr = pltpu.roll(vc, HEAD_DIM, axis=1)
            v_scr[2 * c, k0:k0 + CHUNK, 0:LANES] = jnp.where(lo_c, vc, vr).astype(BF16)
            v_scr[2 * c + 1, k0:k0 + CHUNK, 0:LANES] = jnp.where(lo_c, vr, vc).astype(BF16)
            v_scr[2 * c, k0:k0 + CHUNK, LANES:2 * LANES] = ones
            v_scr[2 * c + 1, k0:k0 + CHUNK, LANES:2 * LANES] = ones

    def q_piece(i, p):
        q = jnp.dot(h_b[i], wq_scr[:, p * piece_w:(p + 1) * piece_w], preferred_element_type=F32)
        cq = cq_ref[i * CHUNK:(i + 1) * CHUNK, :]
        sq = sq_ref[i * CHUNK:(i + 1) * CHUNK, :]
        for gg in range(piece_w // grp_w):
            c0 = q[:, gg * grp_w:gg * grp_w + LANES]
            c1 = q[:, gg * grp_w + LANES:(gg + 1) * grp_w]
            t = c0 * c0 + c1 * c1
            ss = [jnp.sum(jnp.where(quarter_c[hh], t, 0.0), axis=-1, keepdims=True) for hh in range(GROUP)]
            ms = jnp.where(quarter_c[0], ss[0], jnp.where(quarter_c[1], ss[1], jnp.where(quarter_c[2], ss[2], ss[3])))
            rstd = lax.rsqrt(ms * (1.0 / HEAD_DIM) + EPS)
            n0 = c0 * rstd * qga_ref[...]
            n1 = c1 * rstd * qgb_ref[...]
            q_groups[i][p * (piece_w // grp_w) + gg] = jnp.concatenate(
                [n0 * cq - n1 * sq, n1 * cq + n0 * sq], axis=1).astype(BF16)

    def z_piece(i, p):
        z = jnp.dot(h_b[i], winb_ref[:, d_model + p * piece_w:d_model + (p + 1) * piece_w],
                    preferred_element_type=F32)
        gates[i][p] = z * jax.nn.sigmoid(z)

    def o_piece(i, p):
        if gated[i] is None:
            full = jnp.concatenate([jnp.concatenate(attn[i][qb], axis=1) for qb in range(n_qb)], axis=0)
            gated[i] = (full * jnp.concatenate(gates[i], axis=1)).astype(BF16)
        rows = slice(i * CHUNK, (i + 1) * CHUNK)
        cols = slice(p * piece_w, (p + 1) * piece_w)
        o_ref[rows, cols] = x_ref[rows, cols] + jnp.dot(gated[i], woutb_ref[:, cols], preferred_element_type=F32)

    def scores(i, t):
        qb, g = divmod(t, N_KV_HEADS)
        k_lo = i * CHUNK + qb * BLOCK
        qg = q_groups[i][g][qb * BLOCK:(qb + 1) * BLOCK, :]
        q4 = jnp.concatenate([jnp.where(quarter_g[hh], qg, jnp.zeros_like(qg)) for hh in range(GROUP)], axis=0)
        return jnp.dot(q4, kt_scr[g, :, k_lo:k_lo + 2 * BLOCK], preferred_element_type=F32)

    def attend(i, t, s):
        qb, g = divmod(t, N_KV_HEADS)
        k_lo = i * CHUNK + qb * BLOCK
        probs = []
        row_max = []
        for hh in range(GROUP):
            s_prev = s[hh * BLOCK:(hh + 1) * BLOCK, 0:BLOCK]
            s_own = s[hh * BLOCK:(hh + 1) * BLOCK, BLOCK:2 * BLOCK]
            if i == 0 and qb == 0:
                s_prev = jnp.minimum(s_prev, jnp.where(first_tile, NEG_INF, FLT_MAX))
            sm = jnp.where(prev_valid, s_prev, s_own)
            m = jnp.max(sm, axis=-1, keepdims=True)
            p = jnp.exp2(sm - m).astype(BF16)
            zero = jnp.zeros_like(p)
            probs.append(jnp.concatenate([jnp.where(prev_valid, p, zero), jnp.where(prev_valid, zero, p)], axis=1))
            row_max.append(m)
        o4 = jnp.dot(jnp.concatenate(probs, axis=0), v_scr[g, k_lo:k_lo + 2 * BLOCK, :],
                     preferred_element_type=F32)
        for pair in range(GROUP // 2):
            h0, h1 = 2 * pair, 2 * pair + 1
            sink = jnp.where(lo_row, sink_ref[g * GROUP + h0] * LOG2E, sink_ref[g * GROUP + h1] * LOG2E)
            blk0 = o4[h0 * BLOCK:(h0 + 1) * BLOCK, :]
            blk1 = o4[h1 * BLOCK:(h1 + 1) * BLOCK, :]
            den = (jnp.where(lo_b, blk0[:, LANES:2 * LANES], blk1[:, LANES:2 * LANES])
                   + jnp.exp2(sink - jnp.where(lo_b, row_max[h0], row_max[h1])))
            attn[i][qb][2 * g + pair] = jnp.where(lo_b, blk0[:, 0:LANES], blk1[:, 0:LANES]) * (1.0 / den)

    prep(0)
    kv_piece(0)
    for p in range(n_pieces):
        q_piece(0, p)
    for i in range(n_chunks):
        fillers = []
        if i > 0:
            fillers += [functools.partial(o_piece, i - 1, p) for p in range(n_pieces)]
        fillers += [functools.partial(z_piece, i, p) for p in range(n_pieces)]
        if i + 1 < n_chunks:
            prep(i + 1)
            fillers += [functools.partial(kv_piece, i + 1)]
            fillers += [functools.partial(q_piece, i + 1, p) for p in range(n_pieces)]
        after = _spread(n_iter, len(fillers))
        pending = [scores(i, t) for t in range(SCORE_LOOKAHEAD)]
        for t in range(n_iter):
            if t + SCORE_LOOKAHEAD < n_iter:
                pending.append(scores(i, t + SCORE_LOOKAHEAD))
            for k, slot in enumerate(after):
                if slot == t:
                    fillers[k]()
            attend(i, t, pending.pop(0))
    for p in range(n_pieces):
        o_piece(n_chunks - 1, p)


def _rope_tables(seq):
    pos = np.arange(seq, dtype=np.float32)
    inv = np.float32(ROPE_THETA) ** (-np.arange(0, ROT_DIM, 2, dtype=np.float32) / np.float32(ROT_DIM))
    ang = pos[:, None] * inv[None, :]
    return np.cos(ang), np.sin(ang)


def _dim_sets(t):
    mid = ROT_DIM + HALF_DIM - ROT_HALF
    set_a = jnp.concatenate([t[..., 0:ROT_HALF], t[..., ROT_DIM:mid]], axis=-1)
    set_b = jnp.concatenate([t[..., ROT_HALF:ROT_DIM], t[..., mid:HEAD_DIM]], axis=-1)
    return set_a, set_b


def _group_permutation():
    perm = np.zeros((GROUP * HEAD_DIM, GROUP * HEAD_DIM), np.float32)
    for si, dims in enumerate((DIMS_A, DIMS_B)):
        for hh in range(GROUP):
            new = si * GROUP * HALF_DIM + hh * HALF_DIM + np.arange(HALF_DIM)
            perm[hh * HEAD_DIM + dims, new] = 1.0
    return perm.astype(BF16)


def _layer_b(x, ln_kv, w_kv, k_norm, ln_b, w_in_b, q_norm, sinks, w_out_b):
    bsz, seq, d = x.shape
    kv2 = w_kv.shape[1]
    grp_w = GROUP * HEAD_DIM
    scale = LOG2E / math.sqrt(HEAD_DIM)

    cos, sin = _rope_tables(seq)
    cos_t = np.ascontiguousarray(cos.T.reshape(ROT_HALF, seq // CHUNK, CHUNK).transpose(1, 0, 2))
    sin_t = np.ascontiguousarray(sin.T.reshape(ROT_HALF, seq // CHUNK, CHUNK).transpose(1, 0, 2))
    jj = np.arange(LANES) % HALF_DIM
    rot = jj < ROT_HALF
    cq = (np.where(rot, cos[:, jj % ROT_HALF], 1.0) * scale).astype(np.float32)
    sq = (np.where(rot, sin[:, jj % ROT_HALF], 0.0) * scale).astype(np.float32)
    gain_a, gain_b = _dim_sets(q_norm)
    qg_a = jnp.tile(gain_a, GROUP).reshape(1, LANES)
    qg_b = jnp.tile(gain_b, GROUP).reshape(1, LANES)
    perm = _group_permutation()

    const2 = lambda b, j: (0, 0)
    single = pl.Buffered(1)
    return pl.pallas_call(
        _layer_b_kernel,
        out_shape=jax.ShapeDtypeStruct(x.shape, x.dtype),
        grid=(bsz, seq // SEQ_TILE),
        in_specs=[
            pl.BlockSpec((None, SEQ_TILE, d), lambda b, j: (b, j, 0)),
            pl.BlockSpec((1, d), const2),
            pl.BlockSpec((1, d), const2),
            pl.BlockSpec((d, kv2), const2, pipeline_mode=single),
            pl.BlockSpec((HEAD_DIM, 1), const2),
            pl.BlockSpec((SEQ_TILE // CHUNK, ROT_HALF, CHUNK), lambda b, j: (j, 0, 0)),
            pl.BlockSpec((SEQ_TILE // CHUNK, ROT_HALF, CHUNK), lambda b, j: (j, 0, 0)),
            pl.BlockSpec((d, 2 * d), const2, pipeline_mode=single),
            pl.BlockSpec((grp_w, grp_w), const2),
            pl.BlockSpec((1, LANES), const2),
            pl.BlockSpec((1, LANES), const2),
            pl.BlockSpec((SEQ_TILE, LANES), lambda b, j: (j, 0)),
            pl.BlockSpec((SEQ_TILE, LANES), lambda b, j: (j, 0)),
            pl.BlockSpec(memory_space=pltpu.SMEM),
            pl.BlockSpec((d, d), const2, pipeline_mode=single),
        ],
        out_specs=pl.BlockSpec((None, SEQ_TILE, d), lambda b, j: (b, j, 0)),
        scratch_shapes=[
            pltpu.VMEM((N_KV_HEADS, grp_w, BLOCK + SEQ_TILE), BF16),
            pltpu.VMEM((N_KV_HEADS, BLOCK + SEQ_TILE, grp_w), BF16),
            pltpu.VMEM((d, d), BF16),
        ],
        compiler_params=pltpu.CompilerParams(
            dimension_semantics=("arbitrary", "arbitrary"),
            vmem_limit_bytes=VMEM_LIMIT_BYTES),
        name="layer_b_swa",
    )(x, ln_kv.reshape(1, d), ln_b.reshape(1, d), w_kv.astype(BF16), k_norm.reshape(HEAD_DIM, 1),
      cos_t, sin_t, w_in_b.astype(BF16), perm, qg_a, qg_b, cq, sq, sinks.astype(F32), w_out_b.astype(BF16))


def kernel(x, ln_a, w_in_a, conv_a, w_out_a, ln_kv, w_kv, k_norm, ln_b, w_in_b, q_norm, sinks, w_out_b):
    n_a = ln_a.shape[0]
    n_b = ln_b.shape[0]
    assert n_b == 1, "the shared K/V is recomputed inside the single attention layer"
    assert n_a >= 1
    for i in range(n_a):
        nxt = (w_kv, w_in_b[0], w_out_b[0]) if i == n_a - 1 else ()
        x, cast = _layer_a(x, ln_a[i], w_in_a[i], conv_a[i], w_out_a[i], nxt)
    w_kv16, w_in_b16, w_out_b16 = cast
    return _layer_b(x, ln_kv, w_kv16, k_norm, ln_b[0], w_in_b16, q_norm[0], sinks[0], w_out_b16)
```
